```python
import math
import jax, jax.numpy as jnp
from jax import lax
import numpy as np

D_MODEL = 1024
BATCH = 4
SEQ = 8192
DEPTH = 4

PLE_DIM = 256
D_FF = 2816
D_RNN = D_MODEL
RNN_BLOCKS = 16
RNN_BW = D_RNN // RNN_BLOCKS
CONV_W = 4
RG_C = 8.0
N_HEADS = 8
HEAD_DIM = 128
D_ATTN = N_HEADS * HEAD_DIM
Q_BLOCK = 128
N_IN = 2 * D_RNN + 3 * D_ATTN + N_HEADS + 2 * D_MODEL
EPS = 1e-6

kernel_name = "hybrid_rglru_fox_macaron_ple"


def rms_norm(x, g):
    xf = x.astype(jnp.float32)
    y = xf * lax.rsqrt(jnp.mean(xf * xf, axis=-1, keepdims=True) + EPS)
    return (y * g.astype(jnp.float32)).astype(x.dtype)


def swiglu_ffn(h, w_in, w_out):
    g, u = jnp.split(h @ w_in, 2, axis=-1)
    return (jax.nn.silu(g) * u) @ w_out


def causal_dwconv(x, w, b):
    S = x.shape[1]
    xp = jnp.pad(x, ((0, 0), (CONV_W - 1, 0), (0, 0)))
    y = b + xp[:, 0:S] * w[0]
    for k in range(1, CONV_W):
        y = y + xp[:, k:k + S] * w[k]
    return y


def rg_lru(x, w_a, b_a, w_x, b_x, lam):
    B, S, C = x.shape
    xb = x.reshape(B, S, RNN_BLOCKS, RNN_BW)
    r = jax.nn.sigmoid(jnp.einsum('bsnc,ncd->bsnd', xb, w_a).reshape(B, S, C) + b_a)
    i = jax.nn.sigmoid(jnp.einsum('bsnc,ncd->bsnd', xb, w_x).reshape(B, S, C) + b_x)
    log_a = -RG_C * r.astype(jnp.float32) * jax.nn.softplus(-lam.astype(jnp.float32))
    a = jnp.exp(log_a)
    mult = jnp.sqrt(-jnp.expm1(2.0 * log_a))
    u = mult * (i * x).astype(jnp.float32)

    def combine(left, right):
        a1, b1 = left
        a2, b2 = right
        return a1 * a2, a2 * b1 + b2

    _, h = lax.associative_scan(combine, (a, u), axis=1)
    return h.astype(x.dtype)


def forgetting_attention(q, k, v, f_logit, f_b, q_g, k_g):
    B, S, _ = q.shape
    q = rms_norm(q.reshape(B, S, N_HEADS, HEAD_DIM), q_g)
    k = rms_norm(k.reshape(B, S, N_HEADS, HEAD_DIM), k_g)
    v = v.reshape(B, S, N_HEADS, HEAD_DIM)
    log_f = jax.nn.log_sigmoid((f_logit + f_b).astype(jnp.float32))
    dcum = jnp.cumsum(log_f, axis=1).transpose(0, 2, 1)
    qh = q.transpose(0, 2, 1, 3)
    kh = k.transpose(0, 2, 1, 3)
    vh = v.transpose(0, 2, 1, 3)
    nb = S // Q_BLOCK
    q_blocks = qh.reshape(B, N_HEADS, nb, Q_BLOCK, HEAD_DIM).transpose(2, 0, 1, 3, 4)
    d_blocks = dcum.reshape(B, N_HEADS, nb, Q_BLOCK).transpose(2, 0, 1, 3)
    kpos = jnp.arange(S)
    scale = 1.0 / math.sqrt(HEAD_DIM)

    def one_block(args):
        qb, dqb, blk = args
        s = jnp.einsum('bhqd,bhkd->bhqk', qb, kh).astype(jnp.float32) * scale
        s = s + dqb[..., None] - dcum[:, :, None, :]
        qpos = blk * Q_BLOCK + jnp.arange(Q_BLOCK)
        s = jnp.where(kpos[None, :] <= qpos[:, None], s, -jnp.inf)
        pr = jax.nn.softmax(s, axis=-1).astype(vh.dtype)
        return jnp.einsum('bhqk,bhkd->bhqd', pr, vh)

    o = lax.map(one_block, (q_blocks, d_blocks, jnp.arange(nb)))
    return o.transpose(1, 0, 3, 2, 4).reshape(B, S, D_ATTN)


def setup_inputs(seed: int = 0) -> dict:
    key = jax.random.key(seed)
    ks = jax.random.split(key, 32)
    f32 = jnp.float32

    def nrm(k, shape, fan_in):
        return jax.random.normal(k, shape, f32) * (fan_in ** -0.5)

    def gain(k, shape):
        return 1.0 + 0.05 * jax.random.normal(k, shape, f32)

    def small(k, shape):
        return 0.02 * jax.random.normal(k, shape, f32)

    a_c = jax.random.uniform(ks[12], (DEPTH, D_RNN), f32, 0.9, 0.999)
    s = a_c ** (1.0 / RG_C)
    rg_lambda = jnp.log(s) - jnp.log1p(-s)

    return {
        "x": jax.random.normal(ks[0], (BATCH, SEQ, D_MODEL), f32),
        "p": jax.random.normal(ks[1], (DEPTH, BATCH, SEQ, PLE_DIM), f32),
        "ffn1_norm": gain(ks[2], (DEPTH, D_MODEL)),
        "ffn1_w_in": nrm(ks[3], (DEPTH, D_MODEL, 2 * D_FF), D_MODEL),
        "ffn1_w_out": nrm(ks[4], (DEPTH, D_FF, D_MODEL), D_FF),
        "mix_norm": gain(ks[5], (DEPTH, D_MODEL)),
        "w_in": nrm(ks[6], (DEPTH, D_MODEL, N_IN), D_MODEL),
        "merge_b": small(ks[7], (DEPTH, 2 * D_MODEL)),
        "conv_w": nrm(ks[8], (DEPTH, CONV_W, D_RNN), CONV_W),
        "conv_b": small(ks[9], (DEPTH, D_RNN)),
        "rg_wa": nrm(ks[10], (DEPTH, RNN_BLOCKS, RNN_BW, RNN_BW), RNN_BW),
        "rg_ba": small(ks[11], (DEPTH, D_RNN)),
        "rg_wx": nrm(ks[13], (DEPTH, RNN_BLOCKS, RNN_BW, RNN_BW), RNN_BW),
        "rg_bx": small(ks[14], (DEPTH, D_RNN)),
        "rg_lambda": rg_lambda,
        "f_b": jax.random.uniform(ks[15], (DEPTH, N_HEADS), f32, 1.0, 4.0),
        "q_norm": gain(ks[16], (DEPTH, HEAD_DIM)),
        "k_norm": gain(ks[17], (DEPTH, HEAD_DIM)),
        "w_rnn_out": nrm(ks[18], (DEPTH, D_RNN, D_MODEL), D_RNN),
        "w_attn_out": nrm(ks[19], (DEPTH, D_ATTN, D_MODEL), D_ATTN),
        "w_o": nrm(ks[20], (DEPTH, D_MODEL, D_MODEL), D_MODEL),
        "ffn2_norm": gain(ks[21], (DEPTH, D_MODEL)),
        "ffn2_w_in": nrm(ks[22], (DEPTH, D_MODEL, 2 * D_FF), D_MODEL),
        "ffn2_w_out": nrm(ks[23], (DEPTH, D_FF, D_MODEL), D_FF),
        "ple_norm": gain(ks[24], (DEPTH, D_MODEL)),
        "ple_w_gate": nrm(ks[25], (DEPTH, D_MODEL, D_MODEL), D_MODEL),
        "ple_b_gate": small(ks[26], (DEPTH, D_MODEL)),
        "ple_w_proj": nrm(ks[27], (DEPTH, PLE_DIM, D_MODEL), PLE_DIM),
        "final_norm": gain(ks[28], (D_MODEL,)),
    }


def reference(x, p, ffn1_norm, ffn1_w_in, ffn1_w_out, mix_norm, w_in, merge_b,
              conv_w, conv_b, rg_wa, rg_ba, rg_wx, rg_bx, rg_lambda, f_b, q_norm, k_norm,
              w_rnn_out, w_attn_out, w_o, ffn2_norm, ffn2_w_in, ffn2_w_out,
              ple_norm, ple_w_gate, ple_b_gate, ple_w_proj, final_norm):
    split_idx = list(np.cumsum([D_RNN, D_RNN, D_ATTN, D_ATTN, D_ATTN, N_HEADS, D_MODEL]))
    for i in range(DEPTH):
        x = x + 0.5 * swiglu_ffn(rms_norm(x, ffn1_norm[i]), ffn1_w_in[i], ffn1_w_out[i])

        h = rms_norm(x, mix_norm[i])
        proj = h @ w_in[i]
        rx, rgate, q, k, v, f_logit, ga, gb = jnp.split(proj, split_idx, axis=-1)

        rx = causal_dwconv(rx, conv_w[i], conv_b[i])
        ya = rg_lru(rx, rg_wa[i], rg_ba[i], rg_wx[i], rg_bx[i], rg_lambda[i])
        ya = (ya * jax.nn.gelu(rgate)) @ w_rnn_out[i]

        yb = forgetting_attention(q, k, v, f_logit, f_b[i], q_norm[i], k_norm[i]) @ w_attn_out[i]

        mb_a, mb_b = jnp.split(merge_b[i], 2)
        merged = jax.nn.sigmoid(ga + mb_a) * ya + jax.nn.sigmoid(gb + mb_b) * yb
        x = x + merged @ w_o[i]

        x = x + 0.5 * swiglu_ffn(rms_norm(x, ffn2_norm[i]), ffn2_w_in[i], ffn2_w_out[i])

        gate = jax.nn.sigmoid(rms_norm(x, ple_norm[i]) @ ple_w_gate[i] + ple_b_gate[i])
        x = x + gate * (p[i] @ ple_w_proj[i])
    return rms_norm(x, final_norm)
```

```python
import functools
import math

import jax
import jax.numpy as jnp
import numpy as np
from jax import lax
from jax.experimental import pallas as pl
from jax.experimental.pallas import tpu as pltpu

D_MODEL = 1024
D_FF = 2816
D_RNN = 1024
RNN_BLOCKS = 16
RNN_BW = 64
CONV_W = 4
RG_C = 8.0
N_HEADS = 8
HEAD_DIM = 128
D_ATTN = N_HEADS * HEAD_DIM
PLE_DIM = 256
EPS = 1e-6

LANES = 128
SUBLANES = 8
MXU_DIM = 256
VMEM_LIMIT = 56 * 1024 * 1024

TM_FFN = 512
TM_MIX = 256
TQ = 512
GATE_GROUP = MXU_DIM // RNN_BW
N_GATE_GROUPS = RNN_BLOCKS // GATE_GROUP
FFN_CHUNKS = ((0, 1024), (1024, 1024), (2048, 768))
N_SPLIT = 3
HEAD_AUG = 2 * HEAD_DIM

BF16 = jnp.bfloat16
F32 = jnp.float32


def _rms(x, g):
    return x * lax.rsqrt(jnp.mean(x * x, axis=-1, keepdims=True) + EPS) * g


def _sigmoid(x):
    return 1.0 / (1.0 + jnp.exp(-x))


def _softplus(x):
    return jnp.maximum(x, 0.0) + jnp.log1p(jnp.exp(-jnp.abs(x)))


def _dot(a, b):
    return jnp.dot(a, b, preferred_element_type=F32)


def _dot_nt(a, b):
    return lax.dot_general(a, b, (((1,), (1,)), ((), ())), preferred_element_type=F32)


def _layer_spec(shape, layer):
    nd = len(shape)
    return pl.BlockSpec((None,) + tuple(shape), lambda *_: (layer,) + (0,) * nd,
                        pipeline_mode=pl.Buffered(1))


def _const_spec(shape):
    nd = len(shape)
    return pl.BlockSpec(tuple(shape), lambda *_: (0,) * nd, pipeline_mode=pl.Buffered(1))


def _ffn_kernel(*refs, with_ple, with_final):
    x_ref, g_ref, win_ref, wout_ref = refs[:4]
    refs = refs[4:]
    if with_ple:
        p_ref, pg_ref, wpg_ref, bpg_ref, wpe_ref = refs[:5]
        refs = refs[5:]
    if with_final:
        fg_ref = refs[0]
        refs = refs[1:]
    o_ref, act_ref = refs

    x = x_ref[...]
    h = _rms(x, g_ref[...]).astype(BF16)
    for c0, cw in FFN_CHUNKS:
        g = _dot(h, win_ref[:, c0:c0 + cw])
        u = _dot(h, win_ref[:, D_FF + c0:D_FF + c0 + cw])
        act_ref[:, c0:c0 + cw] = (g * _sigmoid(g) * u).astype(BF16)
    x = x + 0.5 * _dot(act_ref[...], wout_ref[...])
    if with_ple:
        hp = _rms(x, pg_ref[...]).astype(BF16)
        gate = _sigmoid(_dot(hp, wpg_ref[...]) + bpg_ref[...])
        x = x + gate * _dot(p_ref[...].astype(BF16), wpe_ref[...])
    if with_final:
        x = _rms(x, fg_ref[...])
    o_ref[...] = x


def _ffn(x2, layer, norm, w_in, w_out, ple=None, final_norm=None):
    t = x2.shape[0]
    row = lambda i: (i, 0)
    in_specs = [
        pl.BlockSpec((TM_FFN, D_MODEL), row),
        _layer_spec((1, D_MODEL), layer),
        _layer_spec((D_MODEL, 2 * D_FF), layer),
        _layer_spec((D_FF, D_MODEL), layer),
    ]
    args = [x2, norm, w_in, w_out]
    if ple is not None:
        p2, ple_norm, w_gate, b_gate, w_proj = ple
        in_specs += [
            pl.BlockSpec((None, TM_FFN, PLE_DIM), lambda i: (layer, i, 0)),
            _layer_spec((1, D_MODEL), layer),
            _layer_spec((D_MODEL, D_MODEL), layer),
            _layer_spec((1, D_MODEL), layer),
            _layer_spec((PLE_DIM, D_MODEL), layer),
        ]
        args += [p2, ple_norm, w_gate, b_gate, w_proj]
    if final_norm is not None:
        in_specs.append(_const_spec((1, D_MODEL)))
        args.append(final_norm)
    return pl.pallas_call(
        functools.partial(_ffn_kernel, with_ple=ple is not None, with_final=final_norm is not None),
        grid=(t // TM_FFN,),
        in_specs=in_specs,
        out_specs=pl.BlockSpec((TM_FFN, D_MODEL), row),
        out_shape=jax.ShapeDtypeStruct((t, D_MODEL), F32),
        scratch_shapes=[pltpu.VMEM((TM_FFN, D_FF), BF16)],
        compiler_params=pltpu.CompilerParams(
            dimension_semantics=("arbitrary",), vmem_limit_bytes=VMEM_LIMIT),
        name="ffn",
    )(*args)


def _mix_in_kernel(x_ref, g_ref, w_ref, wvt_ref, wf_ref, cw_ref, cb_ref, wg_ref, ba_ref, bx_ref,
                   lam_ref, fb_ref, qg_ref, kg_ref, mb_ref, sel_ref, lane_ref,
                   ya_ref, qp_ref, kp_ref, vt_ref, sa_ref, sb_ref,
                   cbuf_ref, a_ref, u_ref, hs_ref, hcar_ref, dcar_ref):
    tm = TM_MIX
    halo = SUBLANES

    @pl.when(pl.program_id(1) == 0)
    def _():
        cbuf_ref[0:halo, :] = jnp.zeros((halo, D_RNN), F32)
        hcar_ref[...] = jnp.zeros_like(hcar_ref)
        dcar_ref[...] = jnp.zeros_like(dcar_ref)

    h = _rms(x_ref[...], g_ref[...]).astype(BF16)

    def proj(idx):
        return _dot(h, w_ref[:, idx * D_MODEL:(idx + 1) * D_MODEL])

    cbuf_ref[halo:halo + tm, :] = proj(0)
    xc = cb_ref[...] + cbuf_ref[halo - 3:halo - 3 + tm, :] * cw_ref[0:1, :]
    for k in range(1, CONV_W):
        xc = xc + cbuf_ref[halo - 3 + k:halo - 3 + k + tm, :] * cw_ref[k:k + 1, :]
    cbuf_ref[0:halo, :] = cbuf_ref[tm:tm + halo, :]

    xcb = xc.astype(BF16)
    neg_c_sp = -RG_C * _softplus(-lam_ref[...])
    for gi in range(N_GATE_GROUPS):
        lo, hi = gi * MXU_DIM, (gi + 1) * MXU_DIM
        pre = _dot(xcb[:, lo:hi], wg_ref[gi])
        r = _sigmoid(pre[:, :MXU_DIM] + ba_ref[:, lo:hi])
        ig = _sigmoid(pre[:, MXU_DIM:] + bx_ref[:, lo:hi])
        log_a = neg_c_sp[:, lo:hi] * r
        a_ref[:, lo:hi] = jnp.exp(log_a)
        th = jnp.tanh(log_a)
        u_ref[:, lo:hi] = jnp.sqrt(-2.0 * th / (1.0 - th)) * (ig * xc[:, lo:hi])

    def scan_row(t, hprev):
        hnew = a_ref[pl.ds(t, 1), :] * hprev + u_ref[pl.ds(t, 1), :]
        hs_ref[pl.ds(t, 1), :] = hnew
        return hnew

    hcar_ref[...] = lax.fori_loop(0, tm, scan_row, hcar_ref[...], unroll=8)
    ya_ref[...] = (hs_ref[...] * jax.nn.gelu(proj(1), approximate=True)).astype(BF16)

    fl = _dot(h, wf_ref[...]) + fb_ref[...]
    d = jnp.minimum(fl, 0.0) - jnp.log1p(jnp.exp(-jnp.abs(fl)))
    rows = lax.broadcasted_iota(jnp.int32, d.shape, 0)
    step = 1
    while step < tm:
        d = d + jnp.where(rows >= step, pltpu.roll(d, step, axis=0), 0.0)
        step *= 2
    d = d + dcar_ref[...]
    dcar_ref[...] = d[tm - 1:tm, :]

    bias = jnp.zeros((tm, D_ATTN), F32)
    rem = d
    for c in range(N_SPLIT):
        piece = rem.astype(BF16)
        rem = rem - piece.astype(F32)
        bias = bias + _dot(piece, sel_ref[c])
    qbias = (bias * lane_ref[0:1, :] + lane_ref[1:2, :]).astype(BF16)
    kbias = (bias * lane_ref[2:3, :] + lane_ref[3:4, :]).astype(BF16)

    q = proj(2)
    k = proj(3)
    qg = qg_ref[...] * (1.0 / math.sqrt(HEAD_DIM))
    for hd in range(N_HEADS):
        lo, hi = hd * HEAD_DIM, (hd + 1) * HEAD_DIM
        base = hd * HEAD_AUG
        qp_ref[:, base:base + HEAD_DIM] = _rms(q[:, lo:hi], qg).astype(BF16)
        qp_ref[:, base + HEAD_DIM:base + HEAD_AUG] = qbias[:, lo:hi]
        kp_ref[:, base:base + HEAD_DIM] = _rms(k[:, lo:hi], kg_ref[...]).astype(BF16)
        kp_ref[:, base + HEAD_DIM:base + HEAD_AUG] = kbias[:, lo:hi]

    vt_ref[...] = _dot_nt(wvt_ref[...], h).astype(BF16)
    sa_ref[...] = _sigmoid(proj(4) + mb_ref[:, :D_MODEL]).astype(BF16)
    sb_ref[...] = _sigmoid(proj(5) + mb_ref[:, D_MODEL:]).astype(BF16)


def _mix_in(x3, layer, w):
    b, s, _ = x3.shape
    nt = s // TM_MIX
    tile = lambda width: pl.BlockSpec((None, TM_MIX, width), lambda bi, ti: (bi, ti, 0))
    in_specs = [
        tile(D_MODEL),
        _layer_spec((1, D_MODEL), layer),
        _layer_spec((D_MODEL, 6 * D_MODEL), layer),
        _layer_spec((D_ATTN, D_MODEL), layer),
        _layer_spec((D_MODEL, LANES), layer),
        _layer_spec((CONV_W, D_RNN), layer),
        _layer_spec((1, D_RNN), layer),
        _layer_spec((N_GATE_GROUPS, MXU_DIM, 2 * MXU_DIM), layer),
        _layer_spec((1, D_RNN), layer),
        _layer_spec((1, D_RNN), layer),
        _layer_spec((1, D_RNN), layer),
        _layer_spec((1, LANES), layer),
        _layer_spec((1, HEAD_DIM), layer),
        _layer_spec((1, HEAD_DIM), layer),
        _layer_spec((1, 2 * D_MODEL), layer),
        _const_spec((N_SPLIT, LANES, D_ATTN)),
        _const_spec((SUBLANES, D_ATTN)),
    ]
    out_specs = [
        tile(D_RNN),
        tile(N_HEADS * HEAD_AUG),
        tile(N_HEADS * HEAD_AUG),
        pl.BlockSpec((None, None, D_ATTN, TM_MIX), lambda bi, ti: (bi, ti, 0, 0)),
        tile(D_MODEL),
        tile(D_MODEL),
    ]
    out_shape = [
        jax.ShapeDtypeStruct((b, s, D_RNN), BF16),
        jax.ShapeDtypeStruct((b, s, N_HEADS * HEAD_AUG), BF16),
        jax.ShapeDtypeStruct((b, s, N_HEADS * HEAD_AUG), BF16),
        jax.ShapeDtypeStruct((b, nt, D_ATTN, TM_MIX), BF16),
        jax.ShapeDtypeStruct((b, s, D_MODEL), BF16),
        jax.ShapeDtypeStruct((b, s, D_MODEL), BF16),
    ]
    scratch = [
        pltpu.VMEM((TM_MIX + SUBLANES, D_RNN), F32),
        pltpu.VMEM((TM_MIX, D_RNN), F32),
        pltpu.VMEM((TM_MIX, D_RNN), F32),
        pltpu.VMEM((TM_MIX, D_RNN), F32),
        pltpu.VMEM((1, D_RNN), F32),
        pltpu.VMEM((1, LANES), F32),
    ]
    return pl.pallas_call(
        _mix_in_kernel,
        grid=(b, nt),
        in_specs=in_specs,
        out_specs=out_specs,
        out_shape=out_shape,
        scratch_shapes=scratch,
        compiler_params=pltpu.CompilerParams(
            dimension_semantics=("arbitrary", "arbitrary"), vmem_limit_bytes=VMEM_LIMIT),
        name="mix_in",
    )(x3, *w)


def _attn_kernel(q_ref, k_ref, v_ref, o_ref, m_ref, l_ref, acc_ref):
    tk = TM_MIX
    qi = pl.program_id(2)
    q = q_ref[...]
    m_ref[...] = jnp.full_like(m_ref, -jnp.inf)
    l_ref[...] = jnp.zeros_like(l_ref)
    acc_ref[...] = jnp.zeros_like(acc_ref)

    def chunk(j, masked):
        k = k_ref[pl.ds(pl.multiple_of(j * tk, tk), tk), :]
        s = _dot_nt(k, q)
        if masked:
            kpos = j * tk + lax.broadcasted_iota(jnp.int32, s.shape, 0)
            qpos = qi * TQ + lax.broadcasted_iota(jnp.int32, s.shape, 1)
            s = jnp.where(kpos <= qpos, s, -jnp.inf)
        m_prev = m_ref[...]
        m_new = jnp.maximum(m_prev, jnp.max(s, axis=0, keepdims=True))
        alpha = jnp.exp(m_prev - m_new)
        p = jnp.exp(s - m_new)
        l_ref[...] = alpha * l_ref[...] + jnp.sum(p, axis=0, keepdims=True)
        acc_ref[...] = alpha * acc_ref[...] + _dot(v_ref[j], p.astype(BF16))
        m_ref[...] = m_new

    n_diag = TQ // tk
    n_full = qi * n_diag

    def full_body(j, carry):
        chunk(j, False)
        return carry

    lax.fori_loop(0, n_full, full_body, 0)
    for dj in range(n_diag):
        chunk(n_full + dj, True)
    o_ref[...] = (acc_ref[...] / l_ref[...]).T.astype(BF16)


def _attention(qp, kp, vt):
    b, s, _ = qp.shape
    nt = s // TM_MIX
    return pl.pallas_call(
        _attn_kernel,
        grid=(b, N_HEADS, s // TQ),
        in_specs=[
            pl.BlockSpec((None, TQ, HEAD_AUG), lambda bi, hi, qi: (bi, qi, hi)),
            pl.BlockSpec((None, s, HEAD_AUG), lambda bi, hi, qi: (bi, 0, hi)),
            pl.BlockSpec((None, nt, HEAD_DIM, TM_MIX), lambda bi, hi, qi: (bi, 0, hi, 0)),
        ],
        out_specs=pl.BlockSpec((None, TQ, HEAD_DIM), lambda bi, hi, qi: (bi, qi, hi)),
        out_shape=jax.ShapeDtypeStruct((b, s, D_ATTN), BF16),
        scratch_shapes=[
            pltpu.VMEM((1, TQ), F32),
            pltpu.VMEM((1, TQ), F32),
            pltpu.VMEM((HEAD_DIM, TQ), F32),
        ],
        compiler_params=pltpu.CompilerParams(
            dimension_semantics=("arbitrary", "arbitrary", "arbitrary"),
            vmem_limit_bytes=VMEM_LIMIT),
        name="attention",
    )(qp, kp, vt)


def _mix_out_kernel(x_ref, ya_ref, at_ref, sa_ref, sb_ref, wro_ref, wao_ref, wo_ref, o_ref):
    ya = _dot(ya_ref[...], wro_ref[...])
    yb = _dot(at_ref[...], wao_ref[...])
    merged = sa_ref[...].astype(F32) * ya + sb_ref[...].astype(F32) * yb
    o_ref[...] = x_ref[...] + _dot(merged.astype(BF16), wo_ref[...])


def _mix_out(x2, ya, at, sa, sb, layer, w_ro, w_ao, w_o):
    t = x2.shape[0]
    row = pl.BlockSpec((TM_FFN, D_MODEL), lambda i: (i, 0))
    wspec = _layer_spec((D_MODEL, D_MODEL), layer)
    return pl.pallas_call(
        _mix_out_kernel,
        grid=(t // TM_FFN,),
        in_specs=[row, row, row, row, row, wspec, wspec, wspec],
        out_specs=row,
        out_shape=jax.ShapeDtypeStruct((t, D_MODEL), F32),
        compiler_params=pltpu.CompilerParams(
            dimension_semantics=("arbitrary",), vmem_limit_bytes=VMEM_LIMIT),
        name="mix_out",
    )(x2, ya, at, sa, sb, w_ro, w_ao, w_o)


def _block_diag_gates(w):
    depth = w.shape[0]
    w = w.reshape(depth, N_GATE_GROUPS, GATE_GROUP, RNN_BW, RNN_BW)
    eye = jnp.eye(GATE_GROUP, dtype=w.dtype)
    w = jnp.einsum("lgncd,nm->lgncmd", w, eye)
    return w.reshape(depth, N_GATE_GROUPS, MXU_DIM, MXU_DIM)


def _bias_constants():
    sel = np.zeros((N_SPLIT, LANES, D_ATTN), np.float32)
    lane = np.zeros((SUBLANES, D_ATTN), np.float32)
    for hd in range(N_HEADS):
        for c in range(N_SPLIT):
            sel[c, hd, hd * HEAD_DIM + c] = -1.0
            sel[c, hd, hd * HEAD_DIM + N_SPLIT + c] = 1.0
            lane[0, hd * HEAD_DIM + N_SPLIT + c] = 1.0
            lane[1, hd * HEAD_DIM + c] = 1.0
            lane[2, hd * HEAD_DIM + c] = 1.0
            lane[3, hd * HEAD_DIM + N_SPLIT + c] = 1.0
    return jnp.asarray(sel, BF16), jnp.asarray(lane, F32)


def kernel(x, p, ffn1_norm, ffn1_w_in, ffn1_w_out, mix_norm, w_in, merge_b, conv_w, conv_b, rg_wa, rg_ba, rg_wx, rg_bx, rg_lambda, f_b, q_norm, k_norm, w_rnn_out, w_attn_out, w_o, ffn2_norm, ffn2_w_in, ffn2_w_out, ple_norm, ple_w_gate, ple_b_gate, ple_w_proj, final_norm):
    b, s, _ = x.shape
    depth = w_in.shape[0]
    t = b * s
    row3 = lambda a: a[:, None, :]
    bf = lambda a: a.astype(BF16)

    o_q, o_k, o_v, o_f, o_g = 2 * D_RNN, 2 * D_RNN + D_ATTN, 2 * D_RNN + 2 * D_ATTN, \
        2 * D_RNN + 3 * D_ATTN, 2 * D_RNN + 3 * D_ATTN + N_HEADS
    w_main = bf(jnp.concatenate(
        [w_in[:, :, :o_v], w_in[:, :, o_g:]], axis=-1))
    w_vt = bf(jnp.swapaxes(w_in[:, :, o_v:o_f], 1, 2))
    w_f = bf(jnp.pad(w_in[:, :, o_f:o_g], ((0, 0), (0, 0), (0, LANES - N_HEADS))))
    fb = jnp.pad(f_b, ((0, 0), (0, LANES - N_HEADS)))
    w_gates = bf(jnp.concatenate([_block_diag_gates(rg_wa), _block_diag_gates(rg_wx)], axis=-1))
    sel, lane = _bias_constants()
    mix_w = (row3(mix_norm), w_main, w_vt, w_f, conv_w, row3(conv_b), w_gates, row3(rg_ba),
             row3(rg_bx), row3(rg_lambda), row3(fb), row3(q_norm), row3(k_norm), row3(merge_b),
             sel, lane)

    f1_in, f1_out, f2_in, f2_out = bf(ffn1_w_in), bf(ffn1_w_out), bf(ffn2_w_in), bf(ffn2_w_out)
    w_ro, w_ao, w_oo = bf(w_rnn_out), bf(w_attn_out), bf(w_o)
    ple_w = (p.reshape(depth, t, PLE_DIM), row3(ple_norm), bf(ple_w_gate), row3(ple_b_gate),
             bf(ple_w_proj))
    fin = final_norm[None, :]

    x2 = x.reshape(t, D_MODEL)
    for layer in range(depth):
        x2 = _ffn(x2, layer, row3(ffn1_norm), f1_in, f1_out)
        ya, qp, kp, vt, sa, sb = _mix_in(x2.reshape(b, s, D_MODEL), layer, mix_w)
        at = _attention(qp, kp, vt)
        flat = lambda a: a.reshape(t, D_MODEL)
        x2 = _mix_out(x2, flat(ya), flat(at), flat(sa), flat(sb), layer, w_ro, w_ao, w_oo)
        x2 = _ffn(x2, layer, row3(ffn2_norm), f2_in, f2_out, ple=ple_w,
                  final_norm=fin if layer == depth - 1 else None)
    return x2.reshape(b, s, D_MODEL)
```

```python
import functools
import math

import jax
import jax.numpy as jnp
import numpy as np
from jax import lax
from jax.experimental import pallas as pl
from jax.experimental.pallas import tpu as pltpu

D_MODEL = 1024
D_FF = 2816
D_RNN = 1024
RNN_BLOCKS = 16
RNN_BW = 64
CONV_W = 4
RG_C = 8.0
N_HEADS = 8
HEAD_DIM = 128
D_ATTN = N_HEADS * HEAD_DIM
PLE_DIM = 256
EPS = 1e-6

LANES = 128
SUBLANES = 8
MXU_DIM = 256
VMEM_LIMIT = 56 * 1024 * 1024

TM_FFN = 512
TM_MIX = 256
TQ = 1024
KV = TQ // 2
V_ROWS = HEAD_DIM + 16
LOG2E = math.log2(math.e)
GATE_GROUP = MXU_DIM // RNN_BW
N_GATE_GROUPS = RNN_BLOCKS // GATE_GROUP
FFN_CHUNKS = ((0, 1024), (1024, 1024), (2048, 768))
N_SPLIT = 3
HEAD_AUG = 2 * HEAD_DIM

BF16 = jnp.bfloat16
F32 = jnp.float32


def _rms(x, g):
    return x * lax.rsqrt(jnp.mean(x * x, axis=-1, keepdims=True) + EPS) * g


def _sigmoid(x):
    return 1.0 / (1.0 + jnp.exp(-x))


def _softplus(x):
    return jnp.maximum(x, 0.0) + jnp.log1p(jnp.exp(-jnp.abs(x)))


def _dot(a, b):
    return jnp.dot(a, b, preferred_element_type=F32)


def _dot_nt(a, b):
    return lax.dot_general(a, b, (((1,), (1,)), ((), ())), preferred_element_type=F32)


def _layer_spec(shape, layer):
    nd = len(shape)
    return pl.BlockSpec((None,) + tuple(shape), lambda *_: (layer,) + (0,) * nd,
                        pipeline_mode=pl.Buffered(1))


def _const_spec(shape):
    nd = len(shape)
    return pl.BlockSpec(tuple(shape), lambda *_: (0,) * nd, pipeline_mode=pl.Buffered(1))


def _ffn_kernel(*refs, with_ple, with_final):
    x_ref, g_ref, win_ref, wout_ref = refs[:4]
    refs = refs[4:]
    if with_ple:
        p_ref, pg_ref, wpg_ref, bpg_ref, wpe_ref = refs[:5]
        refs = refs[5:]
    if with_final:
        fg_ref = refs[0]
        refs = refs[1:]
    o_ref, act_ref = refs

    x = x_ref[...]
    h = _rms(x, g_ref[...]).astype(BF16)
    for c0, cw in FFN_CHUNKS:
        g = _dot(h, win_ref[:, c0:c0 + cw])
        u = _dot(h, win_ref[:, D_FF + c0:D_FF + c0 + cw])
        act_ref[:, c0:c0 + cw] = (g * _sigmoid(g) * u).astype(BF16)
    x = x + 0.5 * _dot(act_ref[...], wout_ref[...])
    if with_ple:
        hp = _rms(x, pg_ref[...]).astype(BF16)
        gate = _sigmoid(_dot(hp, wpg_ref[...]) + bpg_ref[...])
        x = x + gate * _dot(p_ref[...].astype(BF16), wpe_ref[...])
    if with_final:
        x = _rms(x, fg_ref[...])
    o_ref[...] = x


def _ffn(x2, layer, norm, w_in, w_out, ple=None, final_norm=None):
    t = x2.shape[0]
    row = lambda i: (i, 0)
    in_specs = [
        pl.BlockSpec((TM_FFN, D_MODEL), row),
        _layer_spec((1, D_MODEL), layer),
        _layer_spec((D_MODEL, 2 * D_FF), layer),
        _layer_spec((D_FF, D_MODEL), layer),
    ]
    args = [x2, norm, w_in, w_out]
    if ple is not None:
        p2, ple_norm, w_gate, b_gate, w_proj = ple
        in_specs += [
            pl.BlockSpec((None, TM_FFN, PLE_DIM), lambda i: (layer, i, 0)),
            _layer_spec((1, D_MODEL), layer),
            _layer_spec((D_MODEL, D_MODEL), layer),
            _layer_spec((1, D_MODEL), layer),
            _layer_spec((PLE_DIM, D_MODEL), layer),
        ]
        args += [p2, ple_norm, w_gate, b_gate, w_proj]
    if final_norm is not None:
        in_specs.append(_const_spec((1, D_MODEL)))
        args.append(final_norm)
    return pl.pallas_call(
        functools.partial(_ffn_kernel, with_ple=ple is not None, with_final=final_norm is not None),
        grid=(t // TM_FFN,),
        in_specs=in_specs,
        out_specs=pl.BlockSpec((TM_FFN, D_MODEL), row),
        out_shape=jax.ShapeDtypeStruct((t, D_MODEL), F32),
        scratch_shapes=[pltpu.VMEM((TM_FFN, D_FF), BF16)],
        compiler_params=pltpu.CompilerParams(
            dimension_semantics=("arbitrary",), vmem_limit_bytes=VMEM_LIMIT),
        name="ffn",
    )(*args)


def _mix_in_kernel(x_ref, g_ref, w_ref, wvt_ref, wf_ref, cw_ref, cb_ref, wg_ref, ba_ref, bx_ref,
                   lam_ref, fb_ref, qg_ref, kg_ref, mb_ref, sel_ref, lane_ref,
                   ya_ref, qp_ref, kp_ref, vt_ref, sa_ref, sb_ref,
                   cbuf_ref, a_ref, u_ref, hs_ref, hcar_ref, dcar_ref):
    tm = TM_MIX
    halo = SUBLANES

    @pl.when(pl.program_id(1) == 0)
    def _():
        cbuf_ref[0:halo, :] = jnp.zeros((halo, D_RNN), F32)
        hcar_ref[...] = jnp.zeros_like(hcar_ref)
        dcar_ref[...] = jnp.zeros_like(dcar_ref)

    h = _rms(x_ref[...], g_ref[...]).astype(BF16)

    def proj(idx):
        return _dot(h, w_ref[:, idx * D_MODEL:(idx + 1) * D_MODEL])

    cbuf_ref[halo:halo + tm, :] = proj(0)
    xc = cb_ref[...] + cbuf_ref[halo - 3:halo - 3 + tm, :] * cw_ref[0:1, :]
    for k in range(1, CONV_W):
        xc = xc + cbuf_ref[halo - 3 + k:halo - 3 + k + tm, :] * cw_ref[k:k + 1, :]
    cbuf_ref[0:halo, :] = cbuf_ref[tm:tm + halo, :]

    xcb = xc.astype(BF16)
    neg_c_sp = -RG_C * _softplus(-lam_ref[...])
    for gi in range(N_GATE_GROUPS):
        lo, hi = gi * MXU_DIM, (gi + 1) * MXU_DIM
        pre = _dot(xcb[:, lo:hi], wg_ref[gi])
        r = _sigmoid(pre[:, :MXU_DIM] + ba_ref[:, lo:hi])
        ig = _sigmoid(pre[:, MXU_DIM:] + bx_ref[:, lo:hi])
        log_a = neg_c_sp[:, lo:hi] * r
        a_ref[:, lo:hi] = jnp.exp(log_a)
        th = jnp.tanh(log_a)
        u_ref[:, lo:hi] = jnp.sqrt(-2.0 * th / (1.0 - th)) * (ig * xc[:, lo:hi])

    def scan_row(t, hprev):
        hnew = a_ref[pl.ds(t, 1), :] * hprev + u_ref[pl.ds(t, 1), :]
        hs_ref[pl.ds(t, 1), :] = hnew
        return hnew

    hcar_ref[...] = lax.fori_loop(0, tm, scan_row, hcar_ref[...], unroll=8)
    ya_ref[...] = (hs_ref[...] * jax.nn.gelu(proj(1), approximate=True)).astype(BF16)

    fl = _dot(h, wf_ref[...]) + fb_ref[...]
    d = jnp.minimum(fl, 0.0) - jnp.log1p(jnp.exp(-jnp.abs(fl)))
    rows = lax.broadcasted_iota(jnp.int32, d.shape, 0)
    step = 1
    while step < tm:
        d = d + jnp.where(rows >= step, pltpu.roll(d, step, axis=0), 0.0)
        step *= 2
    d = d + dcar_ref[...]
    dcar_ref[...] = d[tm - 1:tm, :]

    bias = jnp.zeros((tm, D_ATTN), F32)
    rem = d * LOG2E
    for c in range(N_SPLIT):
        piece = rem.astype(BF16)
        rem = rem - piece.astype(F32)
        bias = bias + _dot(piece, sel_ref[c])
    qbias = (bias * lane_ref[0:1, :] + lane_ref[1:2, :]).astype(BF16)
    kbias = (bias * lane_ref[2:3, :] + lane_ref[3:4, :]).astype(BF16)

    q = proj(2)
    k = proj(3)
    qg = qg_ref[...] * (LOG2E / math.sqrt(HEAD_DIM))
    for hd in range(N_HEADS):
        lo, hi = hd * HEAD_DIM, (hd + 1) * HEAD_DIM
        base = hd * HEAD_AUG
        qp_ref[:, base:base + HEAD_DIM] = _rms(q[:, lo:hi], qg).astype(BF16)
        qp_ref[:, base + HEAD_DIM:base + HEAD_AUG] = qbias[:, lo:hi]
        kp_ref[:, base:base + HEAD_DIM] = _rms(k[:, lo:hi], kg_ref[...]).astype(BF16)
        kp_ref[:, base + HEAD_DIM:base + HEAD_AUG] = kbias[:, lo:hi]

    vt = _dot_nt(wvt_ref[...], h).astype(BF16)
    pad_rows = V_ROWS - HEAD_DIM
    ones_row = (lax.broadcasted_iota(jnp.int32, (pad_rows, tm), 0) == 0).astype(BF16)
    for hd in range(N_HEADS):
        vt_ref[hd * V_ROWS:hd * V_ROWS + HEAD_DIM, :] = vt[hd * HEAD_DIM:(hd + 1) * HEAD_DIM, :]
        vt_ref[hd * V_ROWS + HEAD_DIM:(hd + 1) * V_ROWS, :] = ones_row
    sa_ref[...] = _sigmoid(proj(4) + mb_ref[:, :D_MODEL]).astype(BF16)
    sb_ref[...] = _sigmoid(proj(5) + mb_ref[:, D_MODEL:]).astype(BF16)


def _mix_in(x3, layer, w):
    b, s, _ = x3.shape
    nt = s // TM_MIX
    tiles_per_kv = KV // TM_MIX
    tile = lambda width: pl.BlockSpec((None, TM_MIX, width), lambda bi, ti: (bi, ti, 0))
    in_specs = [
        tile(D_MODEL),
        _layer_spec((1, D_MODEL), layer),
        _layer_spec((D_MODEL, 6 * D_MODEL), layer),
        _layer_spec((D_ATTN, D_MODEL), layer),
        _layer_spec((D_MODEL, LANES), layer),
        _layer_spec((CONV_W, D_RNN), layer),
        _layer_spec((1, D_RNN), layer),
        _layer_spec((N_GATE_GROUPS, MXU_DIM, 2 * MXU_DIM), layer),
        _layer_spec((1, D_RNN), layer),
        _layer_spec((1, D_RNN), layer),
        _layer_spec((1, D_RNN), layer),
        _layer_spec((1, LANES), layer),
        _layer_spec((1, HEAD_DIM), layer),
        _layer_spec((1, HEAD_DIM), layer),
        _layer_spec((1, 2 * D_MODEL), layer),
        _const_spec((N_SPLIT, LANES, D_ATTN)),
        _const_spec((SUBLANES, D_ATTN)),
    ]
    out_specs = [
        tile(D_RNN),
        tile(N_HEADS * HEAD_AUG),
        tile(N_HEADS * HEAD_AUG),
        pl.BlockSpec((None, None, N_HEADS * V_ROWS, TM_MIX),
                     lambda bi, ti: (bi, ti // tiles_per_kv, 0, ti % tiles_per_kv)),
        tile(D_MODEL),
        tile(D_MODEL),
    ]
    out_shape = [
        jax.ShapeDtypeStruct((b, s, D_RNN), BF16),
        jax.ShapeDtypeStruct((b, s, N_HEADS * HEAD_AUG), BF16),
        jax.ShapeDtypeStruct((b, s, N_HEADS * HEAD_AUG), BF16),
        jax.ShapeDtypeStruct((b, s // KV, N_HEADS * V_ROWS, KV), BF16),
        jax.ShapeDtypeStruct((b, s, D_MODEL), BF16),
        jax.ShapeDtypeStruct((b, s, D_MODEL), BF16),
    ]
    scratch = [
        pltpu.VMEM((TM_MIX + SUBLANES, D_RNN), F32),
        pltpu.VMEM((TM_MIX, D_RNN), F32),
        pltpu.VMEM((TM_MIX, D_RNN), F32),
        pltpu.VMEM((TM_MIX, D_RNN), F32),
        pltpu.VMEM((1, D_RNN), F32),
        pltpu.VMEM((1, LANES), F32),
    ]
    return pl.pallas_call(
        _mix_in_kernel,
        grid=(b, nt),
        in_specs=in_specs,
        out_specs=out_specs,
        out_shape=out_shape,
        scratch_shapes=scratch,
        compiler_params=pltpu.CompilerParams(
            dimension_semantics=("arbitrary", "arbitrary"), vmem_limit_bytes=VMEM_LIMIT),
        name="mix_in",
    )(x3, *w)


def _attn_kernel(q_ref, k_ref, v_ref, mask_ref, o_ref, s0_ref, s1_ref, acc_ref):
    qi = pl.program_id(2)
    acc_ref[...] = jnp.zeros_like(acc_ref)

    def scores(c, q_rows=slice(None)):
        k = k_ref[pl.ds(pl.multiple_of(c * KV, KV), KV), :]
        return _dot_nt(k, q_ref[q_rows, :])

    def col_max(s):
        return jnp.max(s, axis=0, keepdims=True)

    def update(s, s_max, c, m_prev, cols=slice(None)):
        m_new = jnp.maximum(m_prev, s_max)
        alpha = jnp.exp2(m_prev - m_new)
        p = jnp.exp2(s - m_new).astype(BF16)
        acc_ref[:, cols] = alpha * acc_ref[:, cols] + _dot(v_ref[c], p)
        return m_new

    s = scores(0)
    s0_ref[...] = s

    def pair(jj, carry):
        m, max0 = carry
        c = 2 * jj
        s = scores(c + 1)
        s1_ref[...] = s
        max1 = col_max(s)
        m = update(s0_ref[...], max0, c, m)
        s = scores(c + 2)
        s0_ref[...] = s
        max0 = col_max(s)
        m = update(s1_ref[...], max1, c + 1, m)
        return m, max0

    m0 = jnp.full((1, TQ), -jnp.inf, F32)
    m, _ = lax.fori_loop(0, qi, pair, (m0, col_max(s)))

    c = 2 * qi
    s1_ref[:, KV:] = scores(c + 1, slice(KV, TQ))
    s = s0_ref[...] + mask_ref[...]
    m = update(s, col_max(s), c, m)
    s = s1_ref[:, KV:] + mask_ref[:, :KV]
    update(s, col_max(s), c + 1, m[:, KV:], slice(KV, TQ))
    acc = acc_ref[...]
    o_ref[...] = (acc[:HEAD_DIM, :] / acc[HEAD_DIM:HEAD_DIM + 1, :]).T.astype(BF16)


def _attention(qp, kp, vt):
    b, s, _ = qp.shape
    rows, cols = np.arange(KV)[:, None], np.arange(TQ)[None, :]
    mask = jnp.asarray(np.where(rows <= cols, 0.0, -np.inf), F32)
    return pl.pallas_call(
        _attn_kernel,
        grid=(b, N_HEADS, s // TQ),
        in_specs=[
            pl.BlockSpec((None, TQ, HEAD_AUG), lambda bi, hi, qi: (bi, qi, hi)),
            pl.BlockSpec((None, s, HEAD_AUG), lambda bi, hi, qi: (bi, 0, hi)),
            pl.BlockSpec((None, s // KV, V_ROWS, KV), lambda bi, hi, qi: (bi, 0, hi, 0)),
            _const_spec((KV, TQ)),
        ],
        out_specs=pl.BlockSpec((None, TQ, HEAD_DIM), lambda bi, hi, qi: (bi, qi, hi)),
        out_shape=jax.ShapeDtypeStruct((b, s, D_ATTN), BF16),
        scratch_shapes=[
            pltpu.VMEM((KV, TQ), F32),
            pltpu.VMEM((KV, TQ), F32),
            pltpu.VMEM((V_ROWS, TQ), F32),
        ],
        compiler_params=pltpu.CompilerParams(
            dimension_semantics=("arbitrary", "arbitrary", "arbitrary"),
            vmem_limit_bytes=VMEM_LIMIT),
        name="attention",
    )(qp, kp, vt, mask)


def _mix_out_kernel(x_ref, ya_ref, at_ref, sa_ref, sb_ref, wro_ref, wao_ref, wo_ref, o_ref):
    ya = _dot(ya_ref[...], wro_ref[...])
    yb = _dot(at_ref[...], wao_ref[...])
    merged = sa_ref[...].astype(F32) * ya + sb_ref[...].astype(F32) * yb
    o_ref[...] = x_ref[...] + _dot(merged.astype(BF16), wo_ref[...])


def _mix_out(x2, ya, at, sa, sb, layer, w_ro, w_ao, w_o):
    t = x2.shape[0]
    row = pl.BlockSpec((TM_FFN, D_MODEL), lambda i: (i, 0))
    wspec = _layer_spec((D_MODEL, D_MODEL), layer)
    return pl.pallas_call(
        _mix_out_kernel,
        grid=(t // TM_FFN,),
        in_specs=[row, row, row, row, row, wspec, wspec, wspec],
        out_specs=row,
        out_shape=jax.ShapeDtypeStruct((t, D_MODEL), F32),
        compiler_params=pltpu.CompilerParams(
            dimension_semantics=("arbitrary",), vmem_limit_bytes=VMEM_LIMIT),
        name="mix_out",
    )(x2, ya, at, sa, sb, w_ro, w_ao, w_o)


def _block_diag_gates(w):
    depth = w.shape[0]
    w = w.reshape(depth, N_GATE_GROUPS, GATE_GROUP, RNN_BW, RNN_BW)
    eye = jnp.eye(GATE_GROUP, dtype=w.dtype)
    w = jnp.einsum("lgncd,nm->lgncmd", w, eye)
    return w.reshape(depth, N_GATE_GROUPS, MXU_DIM, MXU_DIM)


def _bias_constants():
    sel = np.zeros((N_SPLIT, LANES, D_ATTN), np.float32)
    lane = np.zeros((SUBLANES, D_ATTN), np.float32)
    for hd in range(N_HEADS):
        for c in range(N_SPLIT):
            sel[c, hd, hd * HEAD_DIM + c] = -1.0
            sel[c, hd, hd * HEAD_DIM + N_SPLIT + c] = 1.0
            lane[0, hd * HEAD_DIM + N_SPLIT + c] = 1.0
            lane[1, hd * HEAD_DIM + c] = 1.0
            lane[2, hd * HEAD_DIM + c] = 1.0
            lane[3, hd * HEAD_DIM + N_SPLIT + c] = 1.0
    return jnp.asarray(sel, BF16), jnp.asarray(lane, F32)


def kernel(x, p, ffn1_norm, ffn1_w_in, ffn1_w_out, mix_norm, w_in, merge_b, conv_w, conv_b, rg_wa, rg_ba, rg_wx, rg_bx, rg_lambda, f_b, q_norm, k_norm, w_rnn_out, w_attn_out, w_o, ffn2_norm, ffn2_w_in, ffn2_w_out, ple_norm, ple_w_gate, ple_b_gate, ple_w_proj, final_norm):
    b, s, _ = x.shape
    depth = w_in.shape[0]
    t = b * s
    row3 = lambda a: a[:, None, :]
    bf = lambda a: a.astype(BF16)

    o_q, o_k, o_v, o_f, o_g = 2 * D_RNN, 2 * D_RNN + D_ATTN, 2 * D_RNN + 2 * D_ATTN, \
        2 * D_RNN + 3 * D_ATTN, 2 * D_RNN + 3 * D_ATTN + N_HEADS
    w_main = bf(jnp.concatenate(
        [w_in[:, :, :o_v], w_in[:, :, o_g:]], axis=-1))
    w_vt = bf(jnp.swapaxes(w_in[:, :, o_v:o_f], 1, 2))
    w_f = bf(jnp.pad(w_in[:, :, o_f:o_g], ((0, 0), (0, 0), (0, LANES - N_HEADS))))
    fb = jnp.pad(f_b, ((0, 0), (0, LANES - N_HEADS)))
    w_gates = bf(jnp.concatenate([_block_diag_gates(rg_wa), _block_diag_gates(rg_wx)], axis=-1))
    sel, lane = _bias_constants()
    mix_w = (row3(mix_norm), w_main, w_vt, w_f, conv_w, row3(conv_b), w_gates, row3(rg_ba),
             row3(rg_bx), row3(rg_lambda), row3(fb), row3(q_norm), row3(k_norm), row3(merge_b),
             sel, lane)

    f1_in, f1_out, f2_in, f2_out = bf(ffn1_w_in), bf(ffn1_w_out), bf(ffn2_w_in), bf(ffn2_w_out)
    w_ro, w_ao, w_oo = bf(w_rnn_out), bf(w_attn_out), bf(w_o)
    ple_w = (p.reshape(depth, t, PLE_DIM), row3(ple_norm), bf(ple_w_gate), row3(ple_b_gate),
             bf(ple_w_proj))
    fin = final_norm[None, :]

    x2 = x.reshape(t, D_MODEL)
    for layer in range(depth):
        x2 = _ffn(x2, layer, row3(ffn1_norm), f1_in, f1_out)
        ya, qp, kp, vt, sa, sb = _mix_in(x2.reshape(b, s, D_MODEL), layer, mix_w)
        at = _attention(qp, kp, vt)
        flat = lambda a: a.reshape(t, D_MODEL)
        x2 = _mix_out(x2, flat(ya), flat(at), flat(sa), flat(sb), layer, w_ro, w_ao, w_oo)
        x2 = _ffn(x2, layer, row3(ffn2_norm), f2_in, f2_out, ple=ple_w,
                  final_norm=fin if layer == depth - 1 else None)
    return x2.reshape(b, s, D_MODEL)
```

```python
import functools
import math

import jax
import jax.numpy as jnp
import numpy as np
from jax import lax
from jax.experimental import pallas as pl
from jax.experimental.pallas import tpu as pltpu

D_MODEL = 1024
D_FF = 2816
D_RNN = 1024
RNN_BLOCKS = 16
RNN_BW = 64
CONV_W = 4
RG_C = 8.0
N_HEADS = 8
HEAD_DIM = 128
D_ATTN = N_HEADS * HEAD_DIM
PLE_DIM = 256
EPS = 1e-6

LANES = 128
SUBLANES = 8
MXU_DIM = 256
VMEM_LIMIT = 56 * 1024 * 1024

TM_FFN = 512
TM_MIX = 512
TQ = 2048
KV = 512
N_DIAG = TQ // KV
assert N_DIAG == 4
V_ROWS = HEAD_DIM + 16
LOG2E = math.log2(math.e)
GATE_GROUP = MXU_DIM // RNN_BW
N_GATE_GROUPS = RNN_BLOCKS // GATE_GROUP
FFN_CHUNKS = ((0, 1024), (1024, 1024), (2048, 768))
N_SPLIT = 3
HEAD_AUG = 2 * HEAD_DIM

BF16 = jnp.bfloat16
F32 = jnp.float32


def _rms(x, g):
    return x * lax.rsqrt(jnp.mean(x * x, axis=-1, keepdims=True) + EPS) * g


def _sigmoid(x):
    return 1.0 / (1.0 + jnp.exp(-x))


def _softplus(x):
    return jnp.maximum(x, 0.0) + jnp.log1p(jnp.exp(-jnp.abs(x)))


def _dot(a, b):
    return jnp.dot(a, b, preferred_element_type=F32)


def _dot_nt(a, b):
    return lax.dot_general(a, b, (((1,), (1,)), ((), ())), preferred_element_type=F32)


def _layer_spec(shape, layer):
    nd = len(shape)
    return pl.BlockSpec((None,) + tuple(shape), lambda *_: (layer,) + (0,) * nd,
                        pipeline_mode=pl.Buffered(1))


def _const_spec(shape):
    nd = len(shape)
    return pl.BlockSpec(tuple(shape), lambda *_: (0,) * nd, pipeline_mode=pl.Buffered(1))


def _ffn_kernel(*refs, with_ple, with_final):
    x_ref, g_ref, win_ref, wout_ref = refs[:4]
    refs = refs[4:]
    if with_ple:
        p_ref, pg_ref, wpg_ref, bpg_ref, wpe_ref = refs[:5]
        refs = refs[5:]
    if with_final:
        fg_ref = refs[0]
        refs = refs[1:]
    o_ref, act_ref = refs

    x = x_ref[...]
    h = _rms(x, g_ref[...]).astype(BF16)
    for c0, cw in FFN_CHUNKS:
        g = _dot(h, win_ref[:, c0:c0 + cw])
        u = _dot(h, win_ref[:, D_FF + c0:D_FF + c0 + cw])
        act_ref[:, c0:c0 + cw] = (g * _sigmoid(g) * u).astype(BF16)
    x = x + 0.5 * _dot(act_ref[...], wout_ref[...])
    if with_ple:
        hp = _rms(x, pg_ref[...]).astype(BF16)
        gate = _sigmoid(_dot(hp, wpg_ref[...]) + bpg_ref[...])
        x = x + gate * _dot(p_ref[...].astype(BF16), wpe_ref[...])
    if with_final:
        x = _rms(x, fg_ref[...])
    o_ref[...] = x


def _ffn(x2, layer, norm, w_in, w_out, ple=None, final_norm=None):
    t = x2.shape[0]
    row = lambda i: (i, 0)
    in_specs = [
        pl.BlockSpec((TM_FFN, D_MODEL), row),
        _layer_spec((1, D_MODEL), layer),
        _layer_spec((D_MODEL, 2 * D_FF), layer),
        _layer_spec((D_FF, D_MODEL), layer),
    ]
    args = [x2, norm, w_in, w_out]
    if ple is not None:
        p2, ple_norm, w_gate, b_gate, w_proj = ple
        in_specs += [
            pl.BlockSpec((None, TM_FFN, PLE_DIM), lambda i: (layer, i, 0)),
            _layer_spec((1, D_MODEL), layer),
            _layer_spec((D_MODEL, D_MODEL), layer),
            _layer_spec((1, D_MODEL), layer),
            _layer_spec((PLE_DIM, D_MODEL), layer),
        ]
        args += [p2, ple_norm, w_gate, b_gate, w_proj]
    if final_norm is not None:
        in_specs.append(_const_spec((1, D_MODEL)))
        args.append(final_norm)
    return pl.pallas_call(
        functools.partial(_ffn_kernel, with_ple=ple is not None, with_final=final_norm is not None),
        grid=(t // TM_FFN,),
        in_specs=in_specs,
        out_specs=pl.BlockSpec((TM_FFN, D_MODEL), row),
        out_shape=jax.ShapeDtypeStruct((t, D_MODEL), F32),
        scratch_shapes=[pltpu.VMEM((TM_FFN, D_FF), BF16)],
        compiler_params=pltpu.CompilerParams(
            dimension_semantics=("arbitrary",), vmem_limit_bytes=VMEM_LIMIT),
        name="ffn",
    )(*args)


def _rglru_kernel(x_ref, g_ref, w_ref, cw_ref, cb_ref, wg_ref, ba_ref, bx_ref, lam_ref,
                  perm_ref, unperm_ref, ya_ref,
                  cbuf_ref, tail_ref, a_ref, u_ref, hs_ref, p_ref, hcar_ref):
    tm = TM_MIX
    sl = SUBLANES
    halo = (CONV_W - 1) * sl

    @pl.when(pl.program_id(1) == 0)
    def _():
        tail_ref[...] = jnp.zeros_like(tail_ref)
        hcar_ref[...] = jnp.zeros_like(hcar_ref)

    h = _dot(perm_ref[...], _rms(x_ref[...], g_ref[...]).astype(BF16)).astype(BF16)

    def proj(idx):
        return _dot(h, w_ref[:, idx * D_MODEL:(idx + 1) * D_MODEL])

    cbuf_ref[halo:halo + tm, :] = proj(0)
    first = lax.broadcasted_iota(jnp.int32, (sl, D_RNN), 0) == 0
    for gidx in range(CONV_W - 1):
        cur = cbuf_ref[tm + gidx * sl:tm + (gidx + 1) * sl, :]
        prev = tail_ref[gidx * sl:(gidx + 1) * sl, :]
        cbuf_ref[gidx * sl:(gidx + 1) * sl, :] = jnp.where(
            first, pltpu.roll(prev, 1, axis=0), pltpu.roll(cur, 1, axis=0))
        tail_ref[gidx * sl:(gidx + 1) * sl, :] = cur
    xc = cb_ref[...] + cbuf_ref[0:tm, :] * cw_ref[0:1, :]
    for k in range(1, CONV_W):
        xc = xc + cbuf_ref[k * sl:k * sl + tm, :] * cw_ref[k:k + 1, :]

    xcb = xc.astype(BF16)
    neg_c_sp = -RG_C * _softplus(-lam_ref[...])
    for gi in range(N_GATE_GROUPS):
        lo, hi = gi * MXU_DIM, (gi + 1) * MXU_DIM
        pre = _dot(xcb[:, lo:hi], wg_ref[gi])
        r = _sigmoid(pre[:, :MXU_DIM] + ba_ref[:, lo:hi])
        ig = _sigmoid(pre[:, MXU_DIM:] + bx_ref[:, lo:hi])
        log_a = neg_c_sp[:, lo:hi] * r
        a_ref[:, lo:hi] = jnp.exp(log_a)
        th = jnp.tanh(log_a)
        u_ref[:, lo:hi] = jnp.sqrt(-2.0 * th / (1.0 - th)) * (ig * xc[:, lo:hi])

    def scan_group(j, carry):
        hloc, prod = carry
        rows = pl.ds(pl.multiple_of(j * sl, sl), sl)
        a = a_ref[rows, :]
        hloc = a * hloc + u_ref[rows, :]
        prod = a * prod
        hs_ref[rows, :] = hloc
        p_ref[rows, :] = prod
        return hloc, prod

    zeros = jnp.zeros((sl, D_RNN), F32)
    h_end, p_end = lax.fori_loop(0, tm // sl, scan_group, (zeros, zeros + 1.0), unroll=8)
    state = hcar_ref[...]
    entering = []
    for seg in range(sl):
        entering.append(state)
        state = h_end[seg:seg + 1, :] + p_end[seg:seg + 1, :] * state
    hcar_ref[...] = state
    enter = jnp.concatenate(entering, axis=0)
    shape3 = (tm // sl, sl, D_RNN)
    hs = (hs_ref[...].reshape(shape3) + p_ref[...].reshape(shape3) * enter[None]).reshape(tm, D_RNN)
    ya = (hs * jax.nn.gelu(proj(1), approximate=True)).astype(BF16)
    ya_ref[...] = _dot(unperm_ref[...], ya).astype(BF16)


def _attn_prep_kernel(x_ref, g_ref, w_ref, wvt_ref, wf_ref, fb_ref, qg_ref, kg_ref, mb_ref,
                      sel_ref, lane_ref, qp_ref, kp_ref, vt_ref, sa_ref, sb_ref, dcar_ref):
    tm = TM_MIX

    @pl.when(pl.program_id(1) == 0)
    def _():
        dcar_ref[...] = jnp.zeros_like(dcar_ref)

    h = _rms(x_ref[...], g_ref[...]).astype(BF16)

    def proj(idx):
        return _dot(h, w_ref[:, idx * D_MODEL:(idx + 1) * D_MODEL])

    fl = _dot(h, wf_ref[...]) + fb_ref[...]
    d = jnp.minimum(fl, 0.0) - jnp.log1p(jnp.exp(-jnp.abs(fl)))
    rows = lax.broadcasted_iota(jnp.int32, d.shape, 0)
    step = 1
    while step < tm:
        d = d + jnp.where(rows >= step, pltpu.roll(d, step, axis=0), 0.0)
        step *= 2
    d = d + dcar_ref[...]
    dcar_ref[...] = d[tm - 1:tm, :]

    bias = jnp.zeros((tm, D_ATTN), F32)
    rem = d * LOG2E
    for c in range(N_SPLIT):
        piece = rem.astype(BF16)
        rem = rem - piece.astype(F32)
        bias = bias + _dot(piece, sel_ref[c])
    qbias = (bias * lane_ref[0:1, :] + lane_ref[1:2, :]).astype(BF16)
    kbias = (bias * lane_ref[2:3, :] + lane_ref[3:4, :]).astype(BF16)

    q = proj(0)
    k = proj(1)
    qg = qg_ref[...] * (LOG2E / math.sqrt(HEAD_DIM))
    for hd in range(N_HEADS):
        lo, hi = hd * HEAD_DIM, (hd + 1) * HEAD_DIM
        base = hd * HEAD_AUG
        qp_ref[:, base:base + HEAD_DIM] = _rms(q[:, lo:hi], qg).astype(BF16)
        qp_ref[:, base + HEAD_DIM:base + HEAD_AUG] = qbias[:, lo:hi]
        kp_ref[:, base:base + HEAD_DIM] = _rms(k[:, lo:hi], kg_ref[...]).astype(BF16)
        kp_ref[:, base + HEAD_DIM:base + HEAD_AUG] = kbias[:, lo:hi]

    vt = _dot_nt(wvt_ref[...], h).astype(BF16)
    pad_rows = V_ROWS - HEAD_DIM
    ones_row = (lax.broadcasted_iota(jnp.int32, (pad_rows, tm), 0) == 0).astype(BF16)
    for hd in range(N_HEADS):
        vt_ref[hd * V_ROWS:hd * V_ROWS + HEAD_DIM, :] = vt[hd * HEAD_DIM:(hd + 1) * HEAD_DIM, :]
        vt_ref[hd * V_ROWS + HEAD_DIM:(hd + 1) * V_ROWS, :] = ones_row
    sa_ref[...] = _sigmoid(proj(2) + mb_ref[:, :D_MODEL]).astype(BF16)
    sb_ref[...] = _sigmoid(proj(3) + mb_ref[:, D_MODEL:]).astype(BF16)


def _seq_tile(width):
    return pl.BlockSpec((None, TM_MIX, width), lambda bi, ti: (bi, ti, 0))


_SEQ_PARAMS = pltpu.CompilerParams(
    dimension_semantics=("arbitrary", "arbitrary"), vmem_limit_bytes=VMEM_LIMIT)


def _rglru(x3, layer, w):
    b, s, _ = x3.shape
    in_specs = [
        _seq_tile(D_MODEL),
        _layer_spec((1, D_MODEL), layer),
        _layer_spec((D_MODEL, 2 * D_RNN), layer),
        _layer_spec((CONV_W, D_RNN), layer),
        _layer_spec((1, D_RNN), layer),
        _layer_spec((N_GATE_GROUPS, MXU_DIM, 2 * MXU_DIM), layer),
        _layer_spec((1, D_RNN), layer),
        _layer_spec((1, D_RNN), layer),
        _layer_spec((1, D_RNN), layer),
        _const_spec((TM_MIX, TM_MIX)),
        _const_spec((TM_MIX, TM_MIX)),
    ]
    halo = (CONV_W - 1) * SUBLANES
    scratch = [
        pltpu.VMEM((halo + TM_MIX, D_RNN), F32),
        pltpu.VMEM((halo, D_RNN), F32),
        pltpu.VMEM((TM_MIX, D_RNN), F32),
        pltpu.VMEM((TM_MIX, D_RNN), F32),
        pltpu.VMEM((TM_MIX, D_RNN), F32),
        pltpu.VMEM((TM_MIX, D_RNN), F32),
        pltpu.VMEM((1, D_RNN), F32),
    ]
    return pl.pallas_call(
        _rglru_kernel,
        grid=(b, s // TM_MIX),
        in_specs=in_specs,
        out_specs=_seq_tile(D_RNN),
        out_shape=jax.ShapeDtypeStruct((b, s, D_RNN), BF16),
        scratch_shapes=scratch,
        compiler_params=_SEQ_PARAMS,
        name="rglru",
    )(x3, *w)


def _attn_prep(x3, layer, w):
    b, s, _ = x3.shape
    tiles_per_kv = KV // TM_MIX
    in_specs = [
        _seq_tile(D_MODEL),
        _layer_spec((1, D_MODEL), layer),
        _layer_spec((D_MODEL, 4 * D_MODEL), layer),
        _layer_spec((D_ATTN, D_MODEL), layer),
        _layer_spec((D_MODEL, LANES), layer),
        _layer_spec((1, LANES), layer),
        _layer_spec((1, HEAD_DIM), layer),
        _layer_spec((1, HEAD_DIM), layer),
        _layer_spec((1, 2 * D_MODEL), layer),
        _const_spec((N_SPLIT, LANES, D_ATTN)),
        _const_spec((SUBLANES, D_ATTN)),
    ]
    out_specs = [
        _seq_tile(N_HEADS * HEAD_AUG),
        _seq_tile(N_HEADS * HEAD_AUG),
        pl.BlockSpec((None, None, N_HEADS * V_ROWS, TM_MIX),
                     lambda bi, ti: (bi, ti // tiles_per_kv, 0, ti % tiles_per_kv)),
        _seq_tile(D_MODEL),
        _seq_tile(D_MODEL),
    ]
    out_shape = [
        jax.ShapeDtypeStruct((b, s, N_HEADS * HEAD_AUG), BF16),
        jax.ShapeDtypeStruct((b, s, N_HEADS * HEAD_AUG), BF16),
        jax.ShapeDtypeStruct((b, s // KV, N_HEADS * V_ROWS, KV), BF16),
        jax.ShapeDtypeStruct((b, s, D_MODEL), BF16),
        jax.ShapeDtypeStruct((b, s, D_MODEL), BF16),
    ]
    return pl.pallas_call(
        _attn_prep_kernel,
        grid=(b, s // TM_MIX),
        in_specs=in_specs,
        out_specs=out_specs,
        out_shape=out_shape,
        scratch_shapes=[pltpu.VMEM((1, LANES), F32)],
        compiler_params=_SEQ_PARAMS,
        name="attn_prep",
    )(x3, *w)


def _attn_kernel(q_ref, k_ref, v_ref, mask_ref, o_ref, s0_ref, s1_ref, acc_ref):
    qi = pl.program_id(2)
    acc_ref[...] = jnp.zeros_like(acc_ref)

    def scores(c, q_rows=slice(None)):
        k = k_ref[pl.ds(pl.multiple_of(c * KV, KV), KV), :]
        return _dot_nt(k, q_ref[q_rows, :])

    def col_max(s):
        return jnp.max(s, axis=0, keepdims=True)

    def update(s, s_max, c, m_prev, cols=slice(None)):
        m_new = jnp.maximum(m_prev, s_max)
        alpha = jnp.exp2(m_prev - m_new)
        p = jnp.exp2(s - m_new).astype(BF16)
        acc_ref[:, cols] = alpha * acc_ref[:, cols] + _dot(v_ref[c], p)
        return m_new

    s = scores(0)
    s0_ref[...] = s

    def pair(jj, carry):
        m, max0 = carry
        c = 2 * jj
        s = scores(c + 1)
        s1_ref[...] = s
        max1 = col_max(s)
        m = update(s0_ref[...], max0, c, m)
        s = scores(c + 2)
        s0_ref[...] = s
        max0 = col_max(s)
        m = update(s1_ref[...], max1, c + 1, m)
        return m, max0

    def two_pairs(jj, carry):
        return pair(2 * jj + 1, pair(2 * jj, carry))

    m0 = jnp.full((1, TQ), -jnp.inf, F32)
    m, _ = lax.fori_loop(0, qi, two_pairs, (m0, col_max(s)))

    bufs = (s0_ref, s1_ref)
    for d in range(N_DIAG):
        c = N_DIAG * qi + d
        cur, nxt = bufs[d % 2], bufs[(d + 1) % 2]
        if d + 1 < N_DIAG:
            nxt[:, (d + 1) * KV:] = scores(c + 1, slice((d + 1) * KV, TQ))
        cur[:, d * KV:(d + 1) * KV] = cur[:, d * KV:(d + 1) * KV] + mask_ref[...]
        s = cur[:, d * KV:]
        m_cols = update(s, col_max(s), c, m[:, d * KV:], slice(d * KV, TQ))
        m = m_cols if d == 0 else jnp.concatenate([m[:, :d * KV], m_cols], axis=1)
    acc = acc_ref[...]
    o_ref[...] = (acc[:HEAD_DIM, :] / acc[HEAD_DIM:HEAD_DIM + 1, :]).T.astype(BF16)


def _attention(qp, kp, vt):
    b, s, _ = qp.shape
    rows, cols = np.arange(KV)[:, None], np.arange(KV)[None, :]
    mask = jnp.asarray(np.where(rows <= cols, 0.0, -np.inf), F32)
    return pl.pallas_call(
        _attn_kernel,
        grid=(b, N_HEADS, s // TQ),
        in_specs=[
            pl.BlockSpec((None, TQ, HEAD_AUG), lambda bi, hi, qi: (bi, qi, hi)),
            pl.BlockSpec((None, s, HEAD_AUG), lambda bi, hi, qi: (bi, 0, hi)),
            pl.BlockSpec((None, s // KV, V_ROWS, KV), lambda bi, hi, qi: (bi, 0, hi, 0)),
            _const_spec((KV, KV)),
        ],
        out_specs=pl.BlockSpec((None, TQ, HEAD_DIM), lambda bi, hi, qi: (bi, qi, hi)),
        out_shape=jax.ShapeDtypeStruct((b, s, D_ATTN), BF16),
        scratch_shapes=[
            pltpu.VMEM((KV, TQ), F32),
            pltpu.VMEM((KV, TQ), F32),
            pltpu.VMEM((V_ROWS, TQ), F32),
        ],
        compiler_params=pltpu.CompilerParams(
            dimension_semantics=("arbitrary", "arbitrary", "arbitrary"),
            vmem_limit_bytes=VMEM_LIMIT),
        name="attention",
    )(qp, kp, vt, mask)


def _mix_out_kernel(x_ref, ya_ref, at_ref, sa_ref, sb_ref, wro_ref, wao_ref, wo_ref, o_ref):
    ya = _dot(ya_ref[...], wro_ref[...])
    yb = _dot(at_ref[...], wao_ref[...])
    merged = sa_ref[...].astype(F32) * ya + sb_ref[...].astype(F32) * yb
    o_ref[...] = x_ref[...] + _dot(merged.astype(BF16), wo_ref[...])


def _mix_out(x2, ya, at, sa, sb, layer, w_ro, w_ao, w_o):
    t = x2.shape[0]
    row = pl.BlockSpec((TM_FFN, D_MODEL), lambda i: (i, 0))
    wspec = _layer_spec((D_MODEL, D_MODEL), layer)
    return pl.pallas_call(
        _mix_out_kernel,
        grid=(t // TM_FFN,),
        in_specs=[row, row, row, row, row, wspec, wspec, wspec],
        out_specs=row,
        out_shape=jax.ShapeDtypeStruct((t, D_MODEL), F32),
        compiler_params=pltpu.CompilerParams(
            dimension_semantics=("arbitrary",), vmem_limit_bytes=VMEM_LIMIT),
        name="mix_out",
    )(x2, ya, at, sa, sb, w_ro, w_ao, w_o)


def _block_diag_gates(w):
    depth = w.shape[0]
    w = w.reshape(depth, N_GATE_GROUPS, GATE_GROUP, RNN_BW, RNN_BW)
    eye = jnp.eye(GATE_GROUP, dtype=w.dtype)
    w = jnp.einsum("lgncd,nm->lgncmd", w, eye)
    return w.reshape(depth, N_GATE_GROUPS, MXU_DIM, MXU_DIM)


def _segment_permutation():
    seg_len = TM_MIX // SUBLANES
    r = np.arange(TM_MIX)
    perm = np.zeros((TM_MIX, TM_MIX), np.float32)
    perm[(r % seg_len) * SUBLANES + r // seg_len, r] = 1.0
    return jnp.asarray(perm, BF16), jnp.asarray(perm.T, BF16)


def _bias_constants():
    sel = np.zeros((N_SPLIT, LANES, D_ATTN), np.float32)
    lane = np.zeros((SUBLANES, D_ATTN), np.float32)
    for hd in range(N_HEADS):
        for c in range(N_SPLIT):
            sel[c, hd, hd * HEAD_DIM + c] = -1.0
            sel[c, hd, hd * HEAD_DIM + N_SPLIT + c] = 1.0
            lane[0, hd * HEAD_DIM + N_SPLIT + c] = 1.0
            lane[1, hd * HEAD_DIM + c] = 1.0
            lane[2, hd * HEAD_DIM + c] = 1.0
            lane[3, hd * HEAD_DIM + N_SPLIT + c] = 1.0
    return jnp.asarray(sel, BF16), jnp.asarray(lane, F32)


def kernel(x, p, ffn1_norm, ffn1_w_in, ffn1_w_out, mix_norm, w_in, merge_b, conv_w, conv_b, rg_wa, rg_ba, rg_wx, rg_bx, rg_lambda, f_b, q_norm, k_norm, w_rnn_out, w_attn_out, w_o, ffn2_norm, ffn2_w_in, ffn2_w_out, ple_norm, ple_w_gate, ple_b_gate, ple_w_proj, final_norm):
    b, s, _ = x.shape
    depth = w_in.shape[0]
    t = b * s
    row3 = lambda a: a[:, None, :]
    bf = lambda a: a.astype(BF16)

    o_q, o_k, o_v, o_f, o_g = 2 * D_RNN, 2 * D_RNN + D_ATTN, 2 * D_RNN + 2 * D_ATTN, \
        2 * D_RNN + 3 * D_ATTN, 2 * D_RNN + 3 * D_ATTN + N_HEADS
    w_rg = bf(w_in[:, :, :o_q])
    w_qk = bf(jnp.concatenate([w_in[:, :, o_q:o_v], w_in[:, :, o_g:]], axis=-1))
    w_vt = bf(jnp.swapaxes(w_in[:, :, o_v:o_f], 1, 2))
    w_f = bf(jnp.pad(w_in[:, :, o_f:o_g], ((0, 0), (0, 0), (0, LANES - N_HEADS))))
    fb = jnp.pad(f_b, ((0, 0), (0, LANES - N_HEADS)))
    w_gates = bf(jnp.concatenate([_block_diag_gates(rg_wa), _block_diag_gates(rg_wx)], axis=-1))
    sel, lane = _bias_constants()
    rg_w = (row3(mix_norm), w_rg, conv_w, row3(conv_b), w_gates, row3(rg_ba), row3(rg_bx),
            row3(rg_lambda)) + _segment_permutation()
    prep_w = (row3(mix_norm), w_qk, w_vt, w_f, row3(fb), row3(q_norm), row3(k_norm),
              row3(merge_b), sel, lane)

    f1_in, f1_out, f2_in, f2_out = bf(ffn1_w_in), bf(ffn1_w_out), bf(ffn2_w_in), bf(ffn2_w_out)
    w_ro, w_ao, w_oo = bf(w_rnn_out), bf(w_attn_out), bf(w_o)
    ple_w = (p.reshape(depth, t, PLE_DIM), row3(ple_norm), bf(ple_w_gate), row3(ple_b_gate),
             bf(ple_w_proj))
    fin = final_norm[None, :]

    x2 = x.reshape(t, D_MODEL)
    for layer in range(depth):
        x2 = _ffn(x2, layer, row3(ffn1_norm), f1_in, f1_out)
        x3 = x2.reshape(b, s, D_MODEL)
        ya = _rglru(x3, layer, rg_w)
        qp, kp, vt, sa, sb = _attn_prep(x3, layer, prep_w)
        at = _attention(qp, kp, vt)
        flat = lambda a: a.reshape(t, D_MODEL)
        x2 = _mix_out(x2, flat(ya), flat(at), flat(sa), flat(sb), layer, w_ro, w_ao, w_oo)
        x2 = _ffn(x2, layer, row3(ffn2_norm), f2_in, f2_out, ple=ple_w,
                  final_norm=fin if layer == depth - 1 else None)
    return x2.reshape(b, s, D_MODEL)
```

```python
import functools
import math

import jax
import jax.numpy as jnp
import numpy as np
from jax import lax
from jax.experimental import pallas as pl
from jax.experimental.pallas import tpu as pltpu

D_MODEL = 1024
D_FF = 2816
D_RNN = 1024
RNN_BLOCKS = 16
RNN_BW = 64
CONV_W = 4
RG_C = 8.0
N_HEADS = 8
HEAD_DIM = 128
D_ATTN = N_HEADS * HEAD_DIM
PLE_DIM = 256
EPS = 1e-6

LANES = 128
SUBLANES = 8
MXU_DIM = 256
VMEM_LIMIT = 56 * 1024 * 1024

TM_FFN = 512
TM_MIX = 512
TQ = 2048
KV = 512
N_DIAG = TQ // KV
assert N_DIAG == 4
V_ROWS = HEAD_DIM + 16
LOG2E = math.log2(math.e)
GATE_GROUP = MXU_DIM // RNN_BW
N_GATE_GROUPS = RNN_BLOCKS // GATE_GROUP
FFN_CHUNKS = ((0, 1024), (1024, 1024), (2048, 768))
N_SPLIT = 3
HEAD_AUG = 2 * HEAD_DIM

BF16 = jnp.bfloat16
F32 = jnp.float32


def _rms(x, g):
    return x * lax.rsqrt(jnp.mean(x * x, axis=-1, keepdims=True) + EPS) * g


def _sigmoid(x):
    return 1.0 / (1.0 + jnp.exp(-x))


def _softplus(x):
    return jnp.maximum(x, 0.0) + jnp.log1p(jnp.exp(-jnp.abs(x)))


def _dot(a, b):
    return jnp.dot(a, b, preferred_element_type=F32)


def _dot_nt(a, b):
    return lax.dot_general(a, b, (((1,), (1,)), ((), ())), preferred_element_type=F32)


def _layer_spec(shape, layer):
    nd = len(shape)
    return pl.BlockSpec((None,) + tuple(shape), lambda *_: (layer,) + (0,) * nd,
                        pipeline_mode=pl.Buffered(1))


def _const_spec(shape):
    nd = len(shape)
    return pl.BlockSpec(tuple(shape), lambda *_: (0,) * nd, pipeline_mode=pl.Buffered(1))


def _ffn_kernel(*refs, with_ple, with_final):
    x_ref, g_ref, win_ref, wout_ref = refs[:4]
    refs = refs[4:]
    if with_ple:
        p_ref, pg_ref, wpg_ref, bpg_ref, wpe_ref = refs[:5]
        refs = refs[5:]
    if with_final:
        fg_ref = refs[0]
        refs = refs[1:]
    o_ref, act_ref = refs

    x = x_ref[...]
    h = _rms(x, g_ref[...]).astype(BF16)
    for c0, cw in FFN_CHUNKS:
        g = _dot(h, win_ref[:, c0:c0 + cw])
        u = _dot(h, win_ref[:, D_FF + c0:D_FF + c0 + cw])
        act_ref[:, c0:c0 + cw] = (g * _sigmoid(g) * u).astype(BF16)
    x = x + 0.5 * _dot(act_ref[...], wout_ref[...])
    if with_ple:
        hp = _rms(x, pg_ref[...]).astype(BF16)
        gate = _sigmoid(_dot(hp, wpg_ref[...]) + bpg_ref[...])
        x = x + gate * _dot(p_ref[...].astype(BF16), wpe_ref[...])
    if with_final:
        x = _rms(x, fg_ref[...])
    o_ref[...] = x


def _row_tile(width):
    return pl.BlockSpec((None, TM_FFN, width), lambda bi, ti: (bi, ti, 0))


_ROW_PARAMS = pltpu.CompilerParams(
    dimension_semantics=("arbitrary", "arbitrary"), vmem_limit_bytes=VMEM_LIMIT)


def _ffn(x3, layer, norm, w_in, w_out, ple=None, final_norm=None):
    b, s, _ = x3.shape
    in_specs = [
        _row_tile(D_MODEL),
        _layer_spec((1, D_MODEL), layer),
        _layer_spec((D_MODEL, 2 * D_FF), layer),
        _layer_spec((D_FF, D_MODEL), layer),
    ]
    args = [x3, norm, w_in, w_out]
    if ple is not None:
        p2, ple_norm, w_gate, b_gate, w_proj = ple
        in_specs += [
            pl.BlockSpec((None, None, TM_FFN, PLE_DIM), lambda bi, ti: (layer, bi, ti, 0)),
            _layer_spec((1, D_MODEL), layer),
            _layer_spec((D_MODEL, D_MODEL), layer),
            _layer_spec((1, D_MODEL), layer),
            _layer_spec((PLE_DIM, D_MODEL), layer),
        ]
        args += [p2, ple_norm, w_gate, b_gate, w_proj]
    if final_norm is not None:
        in_specs.append(_const_spec((1, D_MODEL)))
        args.append(final_norm)
    return pl.pallas_call(
        functools.partial(_ffn_kernel, with_ple=ple is not None, with_final=final_norm is not None),
        grid=(b, s // TM_FFN),
        in_specs=in_specs,
        out_specs=_row_tile(D_MODEL),
        out_shape=jax.ShapeDtypeStruct((b, s, D_MODEL), F32),
        scratch_shapes=[pltpu.VMEM((TM_FFN, D_FF), BF16)],
        compiler_params=_ROW_PARAMS,
        name="ffn",
    )(*args)


def _rglru_kernel(x_ref, g_ref, w_ref, cw_ref, cb_ref, wg_ref, ba_ref, bx_ref, lam_ref,
                  perm_ref, unperm_ref, ya_ref,
                  cbuf_ref, tail_ref, a_ref, u_ref, hs_ref, p_ref, hcar_ref):
    tm = TM_MIX
    sl = SUBLANES
    halo = (CONV_W - 1) * sl

    @pl.when(pl.program_id(1) == 0)
    def _():
        tail_ref[...] = jnp.zeros_like(tail_ref)
        hcar_ref[...] = jnp.zeros_like(hcar_ref)

    h = _dot(perm_ref[...], _rms(x_ref[...], g_ref[...]).astype(BF16)).astype(BF16)

    def proj(idx):
        return _dot(h, w_ref[:, idx * D_MODEL:(idx + 1) * D_MODEL])

    cbuf_ref[halo:halo + tm, :] = proj(0)
    first = lax.broadcasted_iota(jnp.int32, (sl, D_RNN), 0) == 0
    for gidx in range(CONV_W - 1):
        cur = cbuf_ref[tm + gidx * sl:tm + (gidx + 1) * sl, :]
        prev = tail_ref[gidx * sl:(gidx + 1) * sl, :]
        cbuf_ref[gidx * sl:(gidx + 1) * sl, :] = jnp.where(
            first, pltpu.roll(prev, 1, axis=0), pltpu.roll(cur, 1, axis=0))
        tail_ref[gidx * sl:(gidx + 1) * sl, :] = cur
    xc = cb_ref[...] + cbuf_ref[0:tm, :] * cw_ref[0:1, :]
    for k in range(1, CONV_W):
        xc = xc + cbuf_ref[k * sl:k * sl + tm, :] * cw_ref[k:k + 1, :]

    xcb = xc.astype(BF16)
    neg_c_sp = -RG_C * _softplus(-lam_ref[...])
    for gi in range(N_GATE_GROUPS):
        lo, hi = gi * MXU_DIM, (gi + 1) * MXU_DIM
        pre = _dot(xcb[:, lo:hi], wg_ref[gi])
        r = _sigmoid(pre[:, :MXU_DIM] + ba_ref[:, lo:hi])
        ig = _sigmoid(pre[:, MXU_DIM:] + bx_ref[:, lo:hi])
        log_a = neg_c_sp[:, lo:hi] * r
        a_ref[:, lo:hi] = jnp.exp(log_a)
        th = jnp.tanh(log_a)
        u_ref[:, lo:hi] = jnp.sqrt(-2.0 * th / (1.0 - th)) * (ig * xc[:, lo:hi])

    def scan_group(j, carry):
        hloc, prod = carry
        rows = pl.ds(pl.multiple_of(j * sl, sl), sl)
        a = a_ref[rows, :]
        hloc = a * hloc + u_ref[rows, :]
        prod = a * prod
        hs_ref[rows, :] = hloc
        p_ref[rows, :] = prod
        return hloc, prod

    zeros = jnp.zeros((sl, D_RNN), F32)
    h_end, p_end = lax.fori_loop(0, tm // sl, scan_group, (zeros, zeros + 1.0), unroll=8)
    state = hcar_ref[...]
    entering = []
    for seg in range(sl):
        entering.append(state)
        state = h_end[seg:seg + 1, :] + p_end[seg:seg + 1, :] * state
    hcar_ref[...] = state
    enter = jnp.concatenate(entering, axis=0)
    shape3 = (tm // sl, sl, D_RNN)
    hs = (hs_ref[...].reshape(shape3) + p_ref[...].reshape(shape3) * enter[None]).reshape(tm, D_RNN)
    ya = (hs * jax.nn.gelu(proj(1), approximate=True)).astype(BF16)
    ya_ref[...] = _dot(unperm_ref[...], ya).astype(BF16)


def _attn_prep_kernel(x_ref, g_ref, w_ref, wvt_ref, wf_ref, fb_ref, qg_ref, kg_ref, mb_ref,
                      sel_ref, lane_ref, qp_ref, kp_ref, vt_ref, sa_ref, sb_ref, dcar_ref):
    tm = TM_MIX

    @pl.when(pl.program_id(1) == 0)
    def _():
        dcar_ref[...] = jnp.zeros_like(dcar_ref)

    h = _rms(x_ref[...], g_ref[...]).astype(BF16)

    def proj(idx):
        return _dot(h, w_ref[:, idx * D_MODEL:(idx + 1) * D_MODEL])

    fl = _dot(h, wf_ref[...]) + fb_ref[...]
    d = jnp.minimum(fl, 0.0) - jnp.log1p(jnp.exp(-jnp.abs(fl)))
    rows = lax.broadcasted_iota(jnp.int32, d.shape, 0)
    step = 1
    while step < tm:
        d = d + jnp.where(rows >= step, pltpu.roll(d, step, axis=0), 0.0)
        step *= 2
    d = d + dcar_ref[...]
    dcar_ref[...] = d[tm - 1:tm, :]

    lanes = lax.broadcasted_iota(jnp.int32, d.shape, 1)
    rem = jnp.where(lanes < N_HEADS, d * LOG2E, 0.0)
    packed = jnp.zeros_like(rem)
    for c in range(N_SPLIT):
        piece = rem.astype(BF16).astype(F32)
        rem = rem - piece
        packed = packed + (piece if c == 0 else pltpu.roll(piece, c * N_HEADS, axis=1))
    bias = _dot(packed.astype(BF16), sel_ref[...])
    qbias = (bias * lane_ref[0:1, :] + lane_ref[1:2, :]).astype(BF16)
    kbias = (bias * lane_ref[2:3, :] + lane_ref[3:4, :]).astype(BF16)

    q = proj(0)
    k = proj(1)
    qg = qg_ref[...] * (LOG2E / math.sqrt(HEAD_DIM))
    for hd in range(N_HEADS):
        lo, hi = hd * HEAD_DIM, (hd + 1) * HEAD_DIM
        base = hd * HEAD_AUG
        qp_ref[:, base:base + HEAD_DIM] = _rms(q[:, lo:hi], qg).astype(BF16)
        qp_ref[:, base + HEAD_DIM:base + HEAD_AUG] = qbias[:, lo:hi]
        kp_ref[:, base:base + HEAD_DIM] = _rms(k[:, lo:hi], kg_ref[...]).astype(BF16)
        kp_ref[:, base + HEAD_DIM:base + HEAD_AUG] = kbias[:, lo:hi]

    vt = _dot_nt(wvt_ref[...], h).astype(BF16)
    pad_rows = V_ROWS - HEAD_DIM
    ones_row = (lax.broadcasted_iota(jnp.int32, (pad_rows, tm), 0) == 0).astype(BF16)
    for hd in range(N_HEADS):
        vt_ref[hd * V_ROWS:hd * V_ROWS + HEAD_DIM, :] = vt[hd * HEAD_DIM:(hd + 1) * HEAD_DIM, :]
        vt_ref[hd * V_ROWS + HEAD_DIM:(hd + 1) * V_ROWS, :] = ones_row
    sa_ref[...] = _sigmoid(proj(2) + mb_ref[:, :D_MODEL]).astype(BF16)
    sb_ref[...] = _sigmoid(proj(3) + mb_ref[:, D_MODEL:]).astype(BF16)


def _seq_tile(width):
    return pl.BlockSpec((None, TM_MIX, width), lambda bi, ti: (bi, ti, 0))


_SEQ_PARAMS = pltpu.CompilerParams(
    dimension_semantics=("arbitrary", "arbitrary"), vmem_limit_bytes=VMEM_LIMIT)


def _rglru(x3, layer, w):
    b, s, _ = x3.shape
    in_specs = [
        _seq_tile(D_MODEL),
        _layer_spec((1, D_MODEL), layer),
        _layer_spec((D_MODEL, 2 * D_RNN), layer),
        _layer_spec((CONV_W, D_RNN), layer),
        _layer_spec((1, D_RNN), layer),
        _layer_spec((N_GATE_GROUPS, MXU_DIM, 2 * MXU_DIM), layer),
        _layer_spec((1, D_RNN), layer),
        _layer_spec((1, D_RNN), layer),
        _layer_spec((1, D_RNN), layer),
        _const_spec((TM_MIX, TM_MIX)),
        _const_spec((TM_MIX, TM_MIX)),
    ]
    halo = (CONV_W - 1) * SUBLANES
    scratch = [
        pltpu.VMEM((halo + TM_MIX, D_RNN), F32),
        pltpu.VMEM((halo, D_RNN), F32),
        pltpu.VMEM((TM_MIX, D_RNN), F32),
        pltpu.VMEM((TM_MIX, D_RNN), F32),
        pltpu.VMEM((TM_MIX, D_RNN), F32),
        pltpu.VMEM((TM_MIX, D_RNN), F32),
        pltpu.VMEM((1, D_RNN), F32),
    ]
    return pl.pallas_call(
        _rglru_kernel,
        grid=(b, s // TM_MIX),
        in_specs=in_specs,
        out_specs=_seq_tile(D_RNN),
        out_shape=jax.ShapeDtypeStruct((b, s, D_RNN), BF16),
        scratch_shapes=scratch,
        compiler_params=_SEQ_PARAMS,
        name="rglru",
    )(x3, *w)


def _attn_prep(x3, layer, w):
    b, s, _ = x3.shape
    tiles_per_kv = KV // TM_MIX
    in_specs = [
        _seq_tile(D_MODEL),
        _layer_spec((1, D_MODEL), layer),
        _layer_spec((D_MODEL, 4 * D_MODEL), layer),
        _layer_spec((D_ATTN, D_MODEL), layer),
        _layer_spec((D_MODEL, LANES), layer),
        _layer_spec((1, LANES), layer),
        _layer_spec((1, HEAD_DIM), layer),
        _layer_spec((1, HEAD_DIM), layer),
        _layer_spec((1, 2 * D_MODEL), layer),
        _const_spec((LANES, D_ATTN)),
        _const_spec((SUBLANES, D_ATTN)),
    ]
    out_specs = [
        _seq_tile(N_HEADS * HEAD_AUG),
        _seq_tile(N_HEADS * HEAD_AUG),
        pl.BlockSpec((None, None, N_HEADS * V_ROWS, TM_MIX),
                     lambda bi, ti: (bi, ti // tiles_per_kv, 0, ti % tiles_per_kv)),
        _seq_tile(D_MODEL),
        _seq_tile(D_MODEL),
    ]
    out_shape = [
        jax.ShapeDtypeStruct((b, s, N_HEADS * HEAD_AUG), BF16),
        jax.ShapeDtypeStruct((b, s, N_HEADS * HEAD_AUG), BF16),
        jax.ShapeDtypeStruct((b, s // KV, N_HEADS * V_ROWS, KV), BF16),
        jax.ShapeDtypeStruct((b, s, D_MODEL), BF16),
        jax.ShapeDtypeStruct((b, s, D_MODEL), BF16),
    ]
    return pl.pallas_call(
        _attn_prep_kernel,
        grid=(b, s // TM_MIX),
        in_specs=in_specs,
        out_specs=out_specs,
        out_shape=out_shape,
        scratch_shapes=[pltpu.VMEM((1, LANES), F32)],
        compiler_params=_SEQ_PARAMS,
        name="attn_prep",
    )(x3, *w)


def _attn_kernel(q_ref, k_ref, v_ref, mask_ref, o_ref, s0_ref, s1_ref, acc_ref):
    qi = pl.program_id(2)
    acc_ref[...] = jnp.zeros_like(acc_ref)

    def scores(c, q_rows=slice(None)):
        k = k_ref[pl.ds(pl.multiple_of(c * KV, KV), KV), :]
        return _dot_nt(k, q_ref[q_rows, :])

    def col_max(s):
        return jnp.max(s, axis=0, keepdims=True)

    def update(s, s_max, c, m_prev, cols=slice(None)):
        m_new = jnp.maximum(m_prev, s_max)
        alpha = jnp.exp2(m_prev - m_new)
        p = jnp.exp2(s - m_new).astype(BF16)
        acc_ref[:, cols] = alpha * acc_ref[:, cols] + _dot(v_ref[c], p)
        return m_new

    s = scores(0)
    s0_ref[...] = s

    def pair(jj, carry):
        m, max0 = carry
        c = 2 * jj
        s = scores(c + 1)
        s1_ref[...] = s
        max1 = col_max(s)
        m = update(s0_ref[...], max0, c, m)
        s = scores(c + 2)
        s0_ref[...] = s
        max0 = col_max(s)
        m = update(s1_ref[...], max1, c + 1, m)
        return m, max0

    def two_pairs(jj, carry):
        return pair(2 * jj + 1, pair(2 * jj, carry))

    m0 = jnp.full((1, TQ), -jnp.inf, F32)
    m, _ = lax.fori_loop(0, qi, two_pairs, (m0, col_max(s)))

    bufs = (s0_ref, s1_ref)
    for d in range(N_DIAG):
        c = N_DIAG * qi + d
        cur, nxt = bufs[d % 2], bufs[(d + 1) % 2]
        if d + 1 < N_DIAG:
            nxt[:, (d + 1) * KV:] = scores(c + 1, slice((d + 1) * KV, TQ))
        cur[:, d * KV:(d + 1) * KV] = cur[:, d * KV:(d + 1) * KV] + mask_ref[...]
        s = cur[:, d * KV:]
        m_cols = update(s, col_max(s), c, m[:, d * KV:], slice(d * KV, TQ))
        m = m_cols if d == 0 else jnp.concatenate([m[:, :d * KV], m_cols], axis=1)
    acc = acc_ref[...]
    o_ref[...] = (acc[:HEAD_DIM, :] / acc[HEAD_DIM:HEAD_DIM + 1, :]).T.astype(BF16)


def _attention(qp, kp, vt):
    b, s, _ = qp.shape
    rows, cols = np.arange(KV)[:, None], np.arange(KV)[None, :]
    mask = jnp.asarray(np.where(rows <= cols, 0.0, -np.inf), F32)
    return pl.pallas_call(
        _attn_kernel,
        grid=(b, N_HEADS, s // TQ),
        in_specs=[
            pl.BlockSpec((None, TQ, HEAD_AUG), lambda bi, hi, qi: (bi, qi, hi)),
            pl.BlockSpec((None, s, HEAD_AUG), lambda bi, hi, qi: (bi, 0, hi)),
            pl.BlockSpec((None, s // KV, V_ROWS, KV), lambda bi, hi, qi: (bi, 0, hi, 0)),
            _const_spec((KV, KV)),
        ],
        out_specs=pl.BlockSpec((None, TQ, HEAD_DIM), lambda bi, hi, qi: (bi, qi, hi)),
        out_shape=jax.ShapeDtypeStruct((b, s, D_ATTN), BF16),
        scratch_shapes=[
            pltpu.VMEM((KV, TQ), F32),
            pltpu.VMEM((KV, TQ), F32),
            pltpu.VMEM((V_ROWS, TQ), F32),
        ],
        compiler_params=pltpu.CompilerParams(
            dimension_semantics=("arbitrary", "arbitrary", "arbitrary"),
            vmem_limit_bytes=VMEM_LIMIT),
        name="attention",
    )(qp, kp, vt, mask)


def _mix_out_kernel(x_ref, ya_ref, at_ref, sa_ref, sb_ref, wro_ref, wao_ref, wo_ref, o_ref):
    ya = _dot(ya_ref[...], wro_ref[...])
    yb = _dot(at_ref[...], wao_ref[...])
    merged = sa_ref[...].astype(F32) * ya + sb_ref[...].astype(F32) * yb
    o_ref[...] = x_ref[...] + _dot(merged.astype(BF16), wo_ref[...])


def _mix_out(x3, ya, at, sa, sb, layer, w_ro, w_ao, w_o):
    b, s, _ = x3.shape
    row = _row_tile(D_MODEL)
    wspec = _layer_spec((D_MODEL, D_MODEL), layer)
    return pl.pallas_call(
        _mix_out_kernel,
        grid=(b, s // TM_FFN),
        in_specs=[row, row, row, row, row, wspec, wspec, wspec],
        out_specs=row,
        out_shape=jax.ShapeDtypeStruct((b, s, D_MODEL), F32),
        compiler_params=_ROW_PARAMS,
        name="mix_out",
    )(x3, ya, at, sa, sb, w_ro, w_ao, w_o)


def _block_diag_gates(w):
    depth = w.shape[0]
    w = w.reshape(depth, N_GATE_GROUPS, GATE_GROUP, RNN_BW, RNN_BW)
    eye = jnp.eye(GATE_GROUP, dtype=w.dtype)
    w = jnp.einsum("lgncd,nm->lgncmd", w, eye)
    return w.reshape(depth, N_GATE_GROUPS, MXU_DIM, MXU_DIM)


def _segment_permutation():
    seg_len = TM_MIX // SUBLANES
    r = np.arange(TM_MIX)
    perm = np.zeros((TM_MIX, TM_MIX), np.float32)
    perm[(r % seg_len) * SUBLANES + r // seg_len, r] = 1.0
    return jnp.asarray(perm, BF16), jnp.asarray(perm.T, BF16)


def _bias_constants():
    sel = np.zeros((LANES, D_ATTN), np.float32)
    lane = np.zeros((SUBLANES, D_ATTN), np.float32)
    for hd in range(N_HEADS):
        for c in range(N_SPLIT):
            sel[c * N_HEADS + hd, hd * HEAD_DIM + c] = -1.0
            sel[c * N_HEADS + hd, hd * HEAD_DIM + N_SPLIT + c] = 1.0
            lane[0, hd * HEAD_DIM + N_SPLIT + c] = 1.0
            lane[1, hd * HEAD_DIM + c] = 1.0
            lane[2, hd * HEAD_DIM + c] = 1.0
            lane[3, hd * HEAD_DIM + N_SPLIT + c] = 1.0
    return jnp.asarray(sel, BF16), jnp.asarray(lane, F32)


def kernel(x, p, ffn1_norm, ffn1_w_in, ffn1_w_out, mix_norm, w_in, merge_b, conv_w, conv_b, rg_wa, rg_ba, rg_wx, rg_bx, rg_lambda, f_b, q_norm, k_norm, w_rnn_out, w_attn_out, w_o, ffn2_norm, ffn2_w_in, ffn2_w_out, ple_norm, ple_w_gate, ple_b_gate, ple_w_proj, final_norm):
    depth = w_in.shape[0]
    row3 = lambda a: a[:, None, :]
    bf = lambda a: a.astype(BF16)

    o_q, o_k, o_v, o_f, o_g = 2 * D_RNN, 2 * D_RNN + D_ATTN, 2 * D_RNN + 2 * D_ATTN, \
        2 * D_RNN + 3 * D_ATTN, 2 * D_RNN + 3 * D_ATTN + N_HEADS
    w_rg = bf(w_in[:, :, :o_q])
    w_qk = bf(jnp.concatenate([w_in[:, :, o_q:o_v], w_in[:, :, o_g:]], axis=-1))
    w_vt = bf(jnp.swapaxes(w_in[:, :, o_v:o_f], 1, 2))
    w_f = bf(jnp.pad(w_in[:, :, o_f:o_g], ((0, 0), (0, 0), (0, LANES - N_HEADS))))
    fb = jnp.pad(f_b, ((0, 0), (0, LANES - N_HEADS)))
    w_gates = bf(jnp.concatenate([_block_diag_gates(rg_wa), _block_diag_gates(rg_wx)], axis=-1))
    sel, lane = _bias_constants()
    rg_w = (row3(mix_norm), w_rg, conv_w, row3(conv_b), w_gates, row3(rg_ba), row3(rg_bx),
            row3(rg_lambda)) + _segment_permutation()
    prep_w = (row3(mix_norm), w_qk, w_vt, w_f, row3(fb), row3(q_norm), row3(k_norm),
              row3(merge_b), sel, lane)

    f1_in, f1_out, f2_in, f2_out = bf(ffn1_w_in), bf(ffn1_w_out), bf(ffn2_w_in), bf(ffn2_w_out)
    w_ro, w_ao, w_oo = bf(w_rnn_out), bf(w_attn_out), bf(w_o)
    ple_w = (p, row3(ple_norm), bf(ple_w_gate), row3(ple_b_gate), bf(ple_w_proj))
    fin = final_norm[None, :]

    for layer in range(depth):
        x = _ffn(x, layer, row3(ffn1_norm), f1_in, f1_out)
        ya = _rglru(x, layer, rg_w)
        qp, kp, vt, sa, sb = _attn_prep(x, layer, prep_w)
        at = _attention(qp, kp, vt)
        x = _mix_out(x, ya, at, sa, sb, layer, w_ro, w_ao, w_oo)
        x = _ffn(x, layer, row3(ffn2_norm), f2_in, f2_out, ple=ple_w,
                 final_norm=fin if layer == depth - 1 else None)
    return x
```
